```python
import math
import jax, jax.numpy as jnp
from jax import lax
import numpy as np

D_MODEL = 4096
BATCH = 4
SEQ = 2048
DEPTH = 1
DEC_BATCH = 32
DEC_SEQ = 1
PAST_LEN = 8192
PAGE_SIZE = 128

MIX_WIDTH = D_MODEL
M_WIDTH = MIX_WIDTH // 2
A_WIDTH = MIX_WIDTH - M_WIDTH
M_HEADS = 4
M_DV = M_WIDTH // M_HEADS
M_DK = M_DV // 2
M_CHUNK = 64
A_HEADS = 8
A_DV = A_WIDTH // A_HEADS
A_DH = A_DV // 2
Q_BLOCK = 128
D_FF = 4 * D_MODEL
N_MOD = 6
EPS = 1e-6
IN_WIDTH = 2 * M_HEADS * M_DK + 2 * M_WIDTH + 2 * M_HEADS + 4 * A_HEADS * A_DH + A_WIDTH

kernel_name = "hybrid_mlstm_diffattn_decode_step"


def rms_norm(x, w):
    xf = x.astype(jnp.float32)
    y = xf * lax.rsqrt(jnp.mean(xf * xf, axis=-1, keepdims=True) + EPS)
    return (y * w.astype(jnp.float32)).astype(x.dtype)


def split_in(proj):
    sizes = (M_HEADS * M_DK, M_HEADS * M_DK, M_WIDTH, M_WIDTH, M_HEADS, M_HEADS,
             A_HEADS * 2 * A_DH, A_HEADS * 2 * A_DH, A_WIDTH)
    points = np.cumsum(sizes)[:-1].tolist()
    return jnp.split(proj, points, axis=-1)


def mlstm_chunk(carry, inp):
    C0, n0, m0 = carry
    q, k, v, ig, lf = inp
    L = q.shape[1]
    bt = jnp.cumsum(lf, axis=1).transpose(0, 2, 1)
    it = ig.transpose(0, 2, 1)
    causal = jnp.tril(jnp.ones((L, L), dtype=bool))
    logD = bt[..., :, None] - bt[..., None, :] + it[..., None, :]
    logD = jnp.where(causal, logD, -jnp.inf)
    m_inter = bt + m0[..., None]
    m = jnp.maximum(m_inter, jnp.max(logD, axis=-1))
    Dm = jnp.exp(logD - m[..., None])
    g = jnp.exp(m_inter - m)
    S = jnp.einsum('blhk,bshk->bhls', q, k) * Dm
    num = (jnp.einsum('bhls,bshv->blhv', S, v)
           + jnp.einsum('blhk,bhkv->blhv', q, C0) * g.transpose(0, 2, 1)[..., None])
    den = jnp.sum(S, axis=-1) + jnp.einsum('blhk,bhk->bhl', q, n0) * g
    den = jnp.maximum(jnp.abs(den), jnp.exp(-m)).transpose(0, 2, 1)
    h = num / den[..., None]
    mL = m[..., -1]
    wL = jnp.exp(bt[..., -1:] - bt + it - mL[..., None])
    gL = jnp.exp(bt[..., -1] + m0 - mL)
    kw = k * wL.transpose(0, 2, 1)[..., None]
    C = gL[..., None, None] * C0 + jnp.einsum('bshk,bshv->bhkv', kw, v)
    n = gL[..., None] * n0 + jnp.sum(kw, axis=1)
    return (C, n, mL), h


def diff_attend(q, k, v, q_pos, k_pos, lam):
    s = jnp.einsum('bqhcd,bkhcd->bhcqk', q, k, preferred_element_type=jnp.float32) * (A_DH ** -0.5)
    mask = k_pos[None, :] <= q_pos[:, None]
    s = jnp.where(mask, s, -jnp.inf)
    p = jax.nn.softmax(s, axis=-1)
    a = p[:, :, 0] - lam * p[:, :, 1]
    return jnp.einsum('bhqk,bkhd->bqhd', a.astype(v.dtype), v)


def trunk_layer(x, c, lp, layer_idx, m_state, kv_past):
    (w_ada, b_ada, norm1_w, w_in, b_igate, b_fgate, mh_norm_w, q_norm_w, k_norm_w,
     lq1, lk1, lq2, lk2, subln_w, w_out, norm2_w, w_up, w_down) = lp
    f32 = jnp.float32
    B, T, _ = x.shape
    mod = (jax.nn.silu(c) @ w_ada + b_ada)[:, None, :]
    sh1, sc1, g1, sh2, sc2, g2 = jnp.split(mod, N_MOD, axis=-1)
    h = rms_norm(x, norm1_w) * (1 + sc1) + sh1
    mq, mk, mv, mo, mi, mf, aq, ak, av = split_in(h @ w_in)

    q = mq.reshape(B, T, M_HEADS, M_DK).astype(f32)
    k = mk.reshape(B, T, M_HEADS, M_DK).astype(f32) * (M_DK ** -0.5)
    v = mv.reshape(B, T, M_HEADS, M_DV).astype(f32)
    ig = mi.astype(f32) + b_igate.astype(f32)
    lf = jax.nn.log_sigmoid(mf.astype(f32) + b_fgate.astype(f32))
    if m_state is None:
        nc = T // M_CHUNK
        carry0 = (jnp.zeros((B, M_HEADS, M_DK, M_DV), f32),
                  jnp.zeros((B, M_HEADS, M_DK), f32),
                  jnp.zeros((B, M_HEADS), f32))
        chunks = tuple(a.reshape((B, nc, M_CHUNK) + a.shape[2:]).swapaxes(0, 1)
                       for a in (q, k, v, ig, lf))
        (C, n, m), hs = lax.scan(mlstm_chunk, carry0, chunks)
        hm = hs.swapaxes(0, 1).reshape(B, T, M_HEADS, M_DV)
    else:
        carry0 = tuple(s.astype(f32) for s in m_state)
        (C, n, m), hm = mlstm_chunk(carry0, (q, k, v, ig, lf))
    hm = rms_norm(hm, mh_norm_w).reshape(B, T, M_WIDTH)
    hm = (jax.nn.sigmoid(mo.astype(f32)) * hm).astype(x.dtype)

    qa = rms_norm(aq.reshape(B, T, A_HEADS, 2, A_DH), q_norm_w)
    ka = rms_norm(ak.reshape(B, T, A_HEADS, 2, A_DH), k_norm_w)
    va = av.reshape(B, T, A_HEADS, A_DV)
    lam_init = 0.8 - 0.6 * math.exp(-0.3 * layer_idx)
    lam = (jnp.exp(jnp.sum(lq1.astype(f32) * lk1.astype(f32)))
           - jnp.exp(jnp.sum(lq2.astype(f32) * lk2.astype(f32))) + lam_init)
    if kv_past is None:
        nqb = T // Q_BLOCK
        qb = qa.reshape(B, nqb, Q_BLOCK, A_HEADS, 2, A_DH).swapaxes(0, 1)
        k_pos = jnp.arange(T)

        def q_block(args):
            qi, bi = args
            return diff_attend(qi, ka, va, bi * Q_BLOCK + jnp.arange(Q_BLOCK), k_pos, lam)

        oa = lax.map(q_block, (qb, jnp.arange(nqb)))
        oa = oa.swapaxes(0, 1).reshape(B, T, A_HEADS, A_DV)
    else:
        k_past, v_past = kv_past
        past = k_past.shape[1]
        k_all = jnp.concatenate([k_past, ka.astype(k_past.dtype)], axis=1)
        v_all = jnp.concatenate([v_past, va.astype(v_past.dtype)], axis=1)
        oa = diff_attend(qa, k_all, v_all, past + jnp.arange(T), jnp.arange(past + T), lam)
    oa = (rms_norm(oa, subln_w) * (1 - lam_init)).reshape(B, T, A_WIDTH).astype(x.dtype)

    mix = jnp.concatenate([hm, oa], axis=-1) @ w_out
    x = x + g1 * mix
    h2 = rms_norm(x, norm2_w) * (1 + sc2) + sh2
    x = x + g2 * (jnp.square(jax.nn.relu(h2 @ w_up)) @ w_down)
    return x, ka, va, C, n, m


def setup_inputs(seed: int = 0) -> dict:
    key = jax.random.key(seed)
    ks = jax.random.split(key, 32)
    f32 = jnp.float32
    n_pages = PAST_LEN // PAGE_SIZE
    n_pool = (DEC_BATCH * n_pages * 5) // 4

    def nrm(k, shape, s=1.0):
        return jax.random.normal(k, shape, f32) * s

    def gain(k, shape):
        return 1.0 + nrm(k, shape, 0.02)

    perm = jax.random.permutation(ks[7], n_pool)
    page_table = perm[:DEC_BATCH * n_pages].reshape(DEC_BATCH, n_pages).astype(jnp.int32)
    return {
        'x_prompt': nrm(ks[0], (BATCH, SEQ, D_MODEL)),
        'x_sample': nrm(ks[1], (DEC_BATCH, DEC_SEQ, D_MODEL)),
        'cache_k': nrm(ks[2], (DEPTH, n_pool, PAGE_SIZE, A_HEADS, 2, A_DH)),
        'cache_v': nrm(ks[3], (DEPTH, n_pool, PAGE_SIZE, A_HEADS, A_DV)),
        'state_C': nrm(ks[4], (DEPTH, DEC_BATCH, M_HEADS, M_DK, M_DV), 0.1),
        'state_n': nrm(ks[5], (DEPTH, DEC_BATCH, M_HEADS, M_DK), 0.5),
        'state_m': nrm(ks[6], (DEPTH, DEC_BATCH, M_HEADS), 0.5),
        'page_table': page_table,
        'c_prompt': nrm(ks[8], (BATCH, D_MODEL)),
        'c_sample': nrm(ks[9], (DEC_BATCH, D_MODEL)),
        'w_ada': nrm(ks[10], (DEPTH, D_MODEL, N_MOD * D_MODEL), 0.5 * D_MODEL ** -0.5),
        'b_ada': nrm(ks[11], (DEPTH, N_MOD * D_MODEL), 0.02),
        'norm1_w': gain(ks[12], (DEPTH, D_MODEL)),
        'w_in': nrm(ks[13], (DEPTH, D_MODEL, IN_WIDTH), D_MODEL ** -0.5),
        'b_igate': nrm(ks[14], (DEPTH, M_HEADS), 0.1),
        'b_fgate': 3.0 + nrm(ks[15], (DEPTH, M_HEADS), 0.5),
        'mh_norm_w': gain(ks[16], (DEPTH, M_HEADS, M_DV)),
        'q_norm_w': gain(ks[17], (DEPTH, A_DH)),
        'k_norm_w': gain(ks[18], (DEPTH, A_DH)),
        'lambda_q1': nrm(ks[19], (DEPTH, A_DH), 0.1),
        'lambda_k1': nrm(ks[20], (DEPTH, A_DH), 0.1),
        'lambda_q2': nrm(ks[21], (DEPTH, A_DH), 0.1),
        'lambda_k2': nrm(ks[22], (DEPTH, A_DH), 0.1),
        'subln_w': gain(ks[23], (DEPTH, A_DV)),
        'w_out': nrm(ks[24], (DEPTH, MIX_WIDTH, D_MODEL), MIX_WIDTH ** -0.5),
        'norm2_w': gain(ks[25], (DEPTH, D_MODEL)),
        'w_up': nrm(ks[26], (DEPTH, D_MODEL, D_FF), D_MODEL ** -0.5),
        'w_down': nrm(ks[27], (DEPTH, D_FF, D_MODEL), D_FF ** -0.5),
    }


def reference(x_prompt, x_sample, cache_k, cache_v, state_C, state_n, state_m, page_table,
              c_prompt, c_sample, w_ada, b_ada, norm1_w, w_in, b_igate, b_fgate, mh_norm_w,
              q_norm_w, k_norm_w, lambda_q1, lambda_k1, lambda_q2, lambda_k2, subln_w,
              w_out, norm2_w, w_up, w_down):
    yp, ys = x_prompt, x_sample
    n_pages = page_table.shape[1]
    kp_l, vp_l, Cp_l, np_l, mp_l = [], [], [], [], []
    ks_l, vs_l, Cs_l, ns_l, ms_l = [], [], [], [], []
    for l in range(DEPTH):
        lp = (w_ada[l], b_ada[l], norm1_w[l], w_in[l], b_igate[l], b_fgate[l], mh_norm_w[l],
              q_norm_w[l], k_norm_w[l], lambda_q1[l], lambda_k1[l], lambda_q2[l], lambda_k2[l],
              subln_w[l], w_out[l], norm2_w[l], w_up[l], w_down[l])
        yp, kp, vp, Cp, np_, mp = trunk_layer(yp, c_prompt, lp, l, None, None)
        k_past = cache_k[l][page_table].reshape(DEC_BATCH if False else page_table.shape[0], n_pages * PAGE_SIZE, A_HEADS, 2, A_DH)
        v_past = cache_v[l][page_table].reshape(page_table.shape[0], n_pages * PAGE_SIZE, A_HEADS, A_DV)
        ys, ks_, vs_, Cs, ns, ms = trunk_layer(ys, c_sample, lp, l,
                                              (state_C[l], state_n[l], state_m[l]), (k_past, v_past))
        kp_l.append(kp); vp_l.append(vp); Cp_l.append(Cp); np_l.append(np_); mp_l.append(mp)
        ks_l.append(ks_); vs_l.append(vs_); Cs_l.append(Cs); ns_l.append(ns); ms_l.append(ms)
    k_prompt = jnp.stack(kp_l); v_prompt = jnp.stack(vp_l)
    C_prompt = jnp.stack(Cp_l); n_prompt = jnp.stack(np_l); m_prompt = jnp.stack(mp_l)
    k_sample = jnp.stack(ks_l); v_sample = jnp.stack(vs_l)
    C_sample = jnp.stack(Cs_l); n_sample = jnp.stack(ns_l); m_sample = jnp.stack(ms_l)
    return (yp, ys, k_prompt, v_prompt, C_prompt, n_prompt, m_prompt,
            k_sample, v_sample, C_sample, n_sample, m_sample)
```

```python
import functools
import math

import jax
import jax.numpy as jnp
from jax import lax
from jax.experimental import pallas as pl
from jax.experimental.pallas import tpu as pltpu

F32 = jnp.float32
BF16 = jnp.bfloat16
EPS = 1e-6
LANE = 128
VMEM_LIMIT = 56 * 1024 * 1024

M_HEADS = 4
A_HEADS = 8
N_MOD = 6
M_CHUNK = 256


def _cparams(sem):
    return pltpu.CompilerParams(dimension_semantics=sem, vmem_limit_bytes=VMEM_LIMIT)


def _dot(a, b):
    return jnp.dot(a, b, preferred_element_type=F32)


def _dot_nt(a, b):
    return lax.dot_general(a, b, (((1,), (1,)), ((), ())), preferred_element_type=F32)


def _dot_tn(a, b):
    return lax.dot_general(a, b, (((0,), (0,)), ((), ())), preferred_element_type=F32)


def _log_sigmoid(x):
    return jnp.minimum(x, 0.0) - jnp.log(1.0 + jnp.exp(-jnp.abs(x)))


def _sigmoid(x):
    return 1.0 / (1.0 + jnp.exp(-x))


def _ada_kernel(c_ref, w_ref, b_ref, o_ref):
    c = c_ref[...]
    a = (c * _sigmoid(c)).astype(BF16)
    o_ref[...] = _dot(a, w_ref[...].astype(BF16)) + b_ref[...]


def _ada(c, w, b, tn=512):
    m, k = c.shape
    n = w.shape[1]
    return pl.pallas_call(
        _ada_kernel,
        grid=(n // tn,),
        in_specs=[pl.BlockSpec((m, k), lambda j: (0, 0)),
                  pl.BlockSpec((k, tn), lambda j: (0, j)),
                  pl.BlockSpec((1, tn), lambda j: (0, j))],
        out_specs=pl.BlockSpec((m, tn), lambda j: (0, j)),
        out_shape=jax.ShapeDtypeStruct((m, n), F32),
        compiler_params=_cparams(("arbitrary",)),
        name="ada",
    )(c, w, b.reshape(1, n))


def _norm_mod_kernel(x_ref, w_ref, sc_ref, sh_ref, o_ref):
    x = x_ref[...]
    sc = sc_ref[...].reshape(-1, x.shape[-1])
    sh = sh_ref[...].reshape(-1, x.shape[-1])
    y = x * lax.rsqrt(jnp.mean(x * x, axis=-1, keepdims=True) + EPS) * w_ref[...]
    o_ref[...] = (y * (1.0 + sc) + sh).astype(o_ref.dtype)


def _norm_mod(x, w, sc, sh, rows_per_mod, tr):
    m, d = x.shape
    if rows_per_mod == 1:
        mod_spec = pl.BlockSpec((tr, d), lambda i: (i, 0))
    else:
        per = rows_per_mod // tr
        sc = sc.reshape(-1, 1, d)
        sh = sh.reshape(-1, 1, d)
        mod_spec = pl.BlockSpec((1, 1, d), lambda i: (i // per, 0, 0))
    return pl.pallas_call(
        _norm_mod_kernel,
        grid=(m // tr,),
        in_specs=[pl.BlockSpec((tr, d), lambda i: (i, 0)),
                  pl.BlockSpec((1, d), lambda i: (0, 0)),
                  mod_spec, mod_spec],
        out_specs=pl.BlockSpec((tr, d), lambda i: (i, 0)),
        out_shape=jax.ShapeDtypeStruct((m, d), BF16),
        compiler_params=_cparams(("arbitrary",)),
        name="norm_mod",
    )(x, w.reshape(1, d), sc, sh)


def _mm_kernel(*refs, n_pairs, epilogue, nk):
    ab = refs[:2 * n_pairs]
    rest = refs[2 * n_pairs:]
    if nk > 1:
        acc_ref = rest[-1]
        rest = rest[:-1]
    o_ref = rest[-1]
    extras = rest[:-1]

    def partial_product():
        r = _dot(ab[0][...], ab[1][...])
        for p in range(1, n_pairs):
            r = r + _dot(ab[2 * p][...], ab[2 * p + 1][...])
        return r

    def finish(acc):
        if epilogue == "plain":
            o_ref[...] = acc.astype(o_ref.dtype)
        elif epilogue == "relu2":
            r = jnp.maximum(acc, 0.0)
            o_ref[...] = (r * r).astype(o_ref.dtype)
        elif epilogue == "qknorm":
            w = extras[0][...]
            for j in range(acc.shape[-1] // LANE):
                blk = acc[:, j * LANE:(j + 1) * LANE]
                y = blk * lax.rsqrt(jnp.mean(blk * blk, axis=-1, keepdims=True) + EPS) * w
                o_ref[:, j * LANE:(j + 1) * LANE] = y.astype(o_ref.dtype)
        elif epilogue == "resid":
            x = extras[0][...]
            g = extras[1][...].reshape(-1, acc.shape[-1])
            o_ref[...] = (x + g * acc).astype(o_ref.dtype)
        else:
            raise ValueError(epilogue)

    if nk == 1:
        finish(partial_product())
    else:
        k = pl.program_id(2)

        @pl.when(k == 0)
        def _():
            acc_ref[...] = partial_product()

        @pl.when(k > 0)
        def _():
            acc_ref[...] += partial_product()

        @pl.when(k == nk - 1)
        def _():
            finish(acc_ref[...])


def _mm(pairs, *, epilogue="plain", out_dtype=F32, tm, tn, tk=None,
        normw=None, resid=None, gate=None, rows_per_gate=1, name="mm"):
    m, kdim = pairs[0][0].shape
    n = pairs[0][1].shape[1]
    tk = kdim if tk is None else tk
    nk = kdim // tk
    assert m % tm == 0 and n % tn == 0 and kdim % tk == 0
    grid = (m // tm, n // tn, nk)
    in_specs, args = [], []
    for a, w in pairs:
        assert a.shape == (m, kdim) and w.shape == (kdim, n)
        in_specs += [pl.BlockSpec((tm, tk), lambda i, j, k: (i, k)),
                     pl.BlockSpec((tk, tn), lambda i, j, k: (k, j))]
        args += [a, w]
    if epilogue == "qknorm":
        in_specs.append(pl.BlockSpec((1, LANE), lambda i, j, k: (0, 0)))
        args.append(normw.reshape(1, LANE))
    if epilogue == "resid":
        in_specs.append(pl.BlockSpec((tm, tn), lambda i, j, k: (i, j)))
        args.append(resid)
        if rows_per_gate == 1:
            in_specs.append(pl.BlockSpec((tm, tn), lambda i, j, k: (i, j)))
            args.append(gate)
        else:
            per = rows_per_gate // tm
            in_specs.append(pl.BlockSpec((1, 1, tn), lambda i, j, k: (i // per, 0, j)))
            args.append(gate.reshape(-1, 1, n))
    scratch = [pltpu.VMEM((tm, tn), F32)] if nk > 1 else []
    return pl.pallas_call(
        functools.partial(_mm_kernel, n_pairs=len(pairs), epilogue=epilogue, nk=nk),
        grid=grid,
        in_specs=in_specs,
        out_specs=pl.BlockSpec((tm, tn), lambda i, j, k: (i, j)),
        out_shape=jax.ShapeDtypeStruct((m, n), out_dtype),
        scratch_shapes=scratch,
        compiler_params=_cparams(("arbitrary", "arbitrary", "arbitrary")),
        name=name,
    )(*args)


def _gate_kernel(h_ref, wc_ref, wr_ref, oc_ref, or_ref):
    h = h_ref[...]
    oc_ref[...] = _dot(h, wc_ref[...])
    or_ref[...] = _dot_nt(wr_ref[...], h)


def _gate_proj(h, w_gate_cols, tm):
    m, k = h.shape
    wc = jnp.pad(w_gate_cols, ((0, 0), (0, LANE - 8))).astype(BF16)
    wr = jnp.pad(w_gate_cols.T, ((0, 8), (0, 0))).astype(BF16)
    return pl.pallas_call(
        _gate_kernel,
        grid=(m // tm,),
        in_specs=[pl.BlockSpec((tm, k), lambda i: (i, 0)),
                  pl.BlockSpec((k, LANE), lambda i: (0, 0)),
                  pl.BlockSpec((16, k), lambda i: (0, 0))],
        out_specs=[pl.BlockSpec((tm, LANE), lambda i: (i, 0)),
                   pl.BlockSpec((16, tm), lambda i: (0, i))],
        out_shape=[jax.ShapeDtypeStruct((m, LANE), F32),
                   jax.ShapeDtypeStruct((16, m), F32)],
        compiler_params=_cparams(("arbitrary",)),
        name="gate_proj",
    )(h, wc, wr)


def _mlstm_prompt_kernel(proj_ref, gc_ref, gr_ref, bc_ref, br_ref, nw_ref,
                         hm_ref, c_out_ref, n_out_ref, m_out_ref,
                         cext_ref, m_ref, *, dk, dv, nc):
    c_idx = pl.program_id(1)
    L = gc_ref.shape[0]
    nh = M_HEADS
    scale = dk ** -0.5

    @pl.when(c_idx == 0)
    def _():
        cext_ref[...] = jnp.zeros_like(cext_ref)
        m_ref[...] = jnp.zeros_like(m_ref)

    row = lax.broadcasted_iota(jnp.int32, (L, L), 0)
    col = lax.broadcasted_iota(jnp.int32, (L, L), 1)
    causal = row >= col
    tril = causal.astype(F32)

    gcol = gc_ref[...] + bc_ref[...]
    grow = gr_ref[...] + br_ref[...]
    lf_col = _log_sigmoid(gcol)
    lf_row = _log_sigmoid(grow)
    bt_col_all = jnp.dot(tril, lf_col, preferred_element_type=F32, precision=lax.Precision.HIGHEST)
    bt_row_all = lax.dot_general(lf_row, tril, (((1,), (1,)), ((), ())),
                                 preferred_element_type=F32, precision=lax.Precision.HIGHEST)

    lane = lax.broadcasted_iota(jnp.int32, (1, LANE), 1)
    m_new_row = jnp.zeros((1, LANE), F32)
    ones_pad = jnp.ones((L, LANE), BF16)

    for h in range(nh):
        it_c = gcol[:, h:h + 1]
        bt_c = bt_col_all[:, nh + h:nh + h + 1]
        it_r = grow[h:h + 1, :]
        bt_r = bt_row_all[nh + h:nh + h + 1, :]
        m0 = m_ref[h][0:1, 0:1]

        log_d = jnp.where(causal, bt_c - bt_r + it_r, -jnp.inf)
        m_inter = bt_c + m0
        m = jnp.maximum(m_inter, jnp.max(log_d, axis=-1, keepdims=True))
        d_m = jnp.exp(log_d - m)
        g = jnp.exp(m_inter - m)

        q = proj_ref[:, h * dk:(h + 1) * dk].astype(BF16)
        k_f = proj_ref[:, nh * dk + h * dk:nh * dk + (h + 1) * dk] * scale
        v_off = 2 * nh * dk
        v = proj_ref[:, v_off + h * dv:v_off + (h + 1) * dv].astype(BF16)
        vext = jnp.concatenate([v, ones_pad], axis=-1)
        o_off = v_off + nh * dv
        o_gate = proj_ref[:, o_off + h * dv:o_off + (h + 1) * dv]

        s = _dot_nt(q, k_f.astype(BF16)) * d_m
        cext = cext_ref[h]
        numext = _dot(s.astype(BF16), vext) + _dot(q, cext.astype(BF16)) * g
        num = numext[:, :dv]
        den = numext[:, dv:dv + 1]
        den = jnp.maximum(jnp.abs(den), jnp.exp(-m))
        hh = num / den
        y = hh * lax.rsqrt(jnp.mean(hh * hh, axis=-1, keepdims=True) + EPS) * nw_ref[:, h * dv:(h + 1) * dv]
        hm_ref[:, h * dv:(h + 1) * dv] = (_sigmoid(o_gate) * y).astype(hm_ref.dtype)

        m_last = m[L - 1:L, :]
        bt_last = bt_c[L - 1:L, :]
        w_c = jnp.exp(bt_last - bt_c + it_c - m_last)
        g_last = jnp.exp(bt_last + m0 - m_last)
        kw = (k_f * w_c).astype(BF16)
        cext_ref[h] = g_last * cext + _dot_tn(kw, vext)
        m_ref[h] = jnp.broadcast_to(m_last, m_ref.shape[1:])
        m_new_row = jnp.where(lane == h, m_last, m_new_row)

    @pl.when(c_idx == nc - 1)
    def _():
        for h in range(nh):
            c_out_ref[0, h] = cext_ref[h][:, :dv]
            n_out_ref[0, h] = cext_ref[h][:, dv:]
        m_out_ref[0] = m_new_row


def _mlstm_prompt(proj, gcol, grow, b_ig, b_fg, mh_norm_w, batch, seq, dk, dv):
    nh = M_HEADS
    L = min(M_CHUNK, seq)
    nc = seq // L
    m = batch * seq
    width = proj.shape[1]
    bias = jnp.concatenate([b_ig, b_fg]).astype(F32)
    bias_cols = jnp.pad(bias, (0, LANE - 2 * nh)).reshape(1, LANE)
    bias_rows = jnp.pad(bias, (0, 16 - 2 * nh)).reshape(16, 1)
    kern = functools.partial(_mlstm_prompt_kernel, dk=dk, dv=dv, nc=nc)
    return pl.pallas_call(
        kern,
        grid=(batch, nc),
        in_specs=[pl.BlockSpec((L, width), lambda b, c: (b * nc + c, 0)),
                  pl.BlockSpec((L, LANE), lambda b, c: (b * nc + c, 0)),
                  pl.BlockSpec((16, L), lambda b, c: (0, b * nc + c)),
                  pl.BlockSpec((1, LANE), lambda b, c: (0, 0)),
                  pl.BlockSpec((16, 1), lambda b, c: (0, 0)),
                  pl.BlockSpec((1, nh * dv), lambda b, c: (0, 0))],
        out_specs=[pl.BlockSpec((L, nh * dv), lambda b, c: (b * nc + c, 0)),
                   pl.BlockSpec((1, nh, dk, dv), lambda b, c: (b, 0, 0, 0)),
                   pl.BlockSpec((1, nh, dk, LANE), lambda b, c: (b, 0, 0, 0)),
                   pl.BlockSpec((1, 1, LANE), lambda b, c: (b, 0, 0))],
        out_shape=[jax.ShapeDtypeStruct((m, nh * dv), BF16),
                   jax.ShapeDtypeStruct((batch, nh, dk, dv), F32),
                   jax.ShapeDtypeStruct((batch, nh, dk, LANE), F32),
                   jax.ShapeDtypeStruct((batch, 1, LANE), F32)],
        scratch_shapes=[pltpu.VMEM((nh, dk, dv + LANE), F32),
                        pltpu.VMEM((nh, 8, LANE), F32)],
        compiler_params=_cparams(("arbitrary", "arbitrary")),
        name="mlstm_prompt",
    )(proj, gcol, grow, bias_cols, bias_rows, mh_norm_w.reshape(1, nh * dv))


def _mlstm_decode_kernel(proj_ref, g_ref, bias_ref, nw_ref, c0_ref, n0_ref, m0_ref,
                         hm_ref, c_out_ref, n_out_ref, m_out_ref, *, dk, dv):
    nh = M_HEADS
    scale = dk ** -0.5
    gates = g_ref[0] + bias_ref[...]
    lf_all = _log_sigmoid(gates)
    m0_all = m0_ref[0]
    lane = lax.broadcasted_iota(jnp.int32, (1, LANE), 1)
    eye = (lax.broadcasted_iota(jnp.int32, (dk, dk), 0)
           == lax.broadcasted_iota(jnp.int32, (dk, dk), 1))
    m_new_row = jnp.zeros((1, LANE), F32)

    def to_col(r):
        return jnp.sum(jnp.where(eye, jnp.broadcast_to(r, (dk, dk)), 0.0), axis=1, keepdims=True)

    v_off = 2 * nh * dk
    o_off = v_off + nh * dv
    for h in range(nh):
        it = gates[:, h:h + 1]
        lf = lf_all[:, nh + h:nh + h + 1]
        m0 = m0_all[:, h:h + 1]
        q = proj_ref[0, :, h * dk:(h + 1) * dk]
        k = proj_ref[0, :, nh * dk + h * dk:nh * dk + (h + 1) * dk] * scale
        v = proj_ref[0, :, v_off + h * dv:v_off + (h + 1) * dv]
        o_gate = proj_ref[0, :, o_off + h * dv:o_off + (h + 1) * dv]
        c0 = c0_ref[0, h]
        n0 = n0_ref[0, h:h + 1, :]

        m_inter = lf + m0
        m = jnp.maximum(m_inter, it)
        d_m = jnp.exp(it - m)
        g = jnp.exp(m_inter - m)
        s = jnp.sum(q * k, axis=-1, keepdims=True) * d_m
        q_col = to_col(q)
        qc = jnp.sum(q_col * c0, axis=0, keepdims=True)
        num = s * v + qc * g
        den = s + jnp.sum(q * n0, axis=-1, keepdims=True) * g
        den = jnp.maximum(jnp.abs(den), jnp.exp(-m))
        hh = num / den
        y = hh * lax.rsqrt(jnp.mean(hh * hh, axis=-1, keepdims=True) + EPS) * nw_ref[:, h * dv:(h + 1) * dv]
        hm_ref[0, :, h * dv:(h + 1) * dv] = (_sigmoid(o_gate) * y).astype(hm_ref.dtype)

        kw = k * d_m
        c_out_ref[0, h] = g * c0 + to_col(kw) * v
        n_out_ref[0, h:h + 1, :] = g * n0 + kw
        m_new_row = jnp.where(lane == h, m, m_new_row)
    m_out_ref[0] = m_new_row


def _mlstm_decode(proj, gcol, b_ig, b_fg, mh_norm_w, c0, n0, m0, dk, dv):
    nh = M_HEADS
    nb, width = proj.shape
    bias = jnp.pad(jnp.concatenate([b_ig, b_fg]).astype(F32), (0, LANE - 2 * nh)).reshape(1, LANE)
    kern = functools.partial(_mlstm_decode_kernel, dk=dk, dv=dv)
    return pl.pallas_call(
        kern,
        grid=(nb,),
        in_specs=[pl.BlockSpec((1, 1, width), lambda b: (b, 0, 0)),
                  pl.BlockSpec((1, 1, LANE), lambda b: (b, 0, 0)),
                  pl.BlockSpec((1, LANE), lambda b: (0, 0)),
                  pl.BlockSpec((1, nh * dv), lambda b: (0, 0)),
                  pl.BlockSpec((1, nh, dk, dv), lambda b: (b, 0, 0, 0)),
                  pl.BlockSpec((1, nh, dk), lambda b: (b, 0, 0)),
                  pl.BlockSpec((1, 1, nh), lambda b: (b, 0, 0))],
        out_specs=[pl.BlockSpec((1, 1, nh * dv), lambda b: (b, 0, 0)),
                   pl.BlockSpec((1, nh, dk, dv), lambda b: (b, 0, 0, 0)),
                   pl.BlockSpec((1, nh, dk), lambda b: (b, 0, 0)),
                   pl.BlockSpec((1, 1, LANE), lambda b: (b, 0, 0))],
        out_shape=[jax.ShapeDtypeStruct((nb, 1, nh * dv), BF16),
                   jax.ShapeDtypeStruct((nb, nh, dk, dv), F32),
                   jax.ShapeDtypeStruct((nb, nh, dk), F32),
                   jax.ShapeDtypeStruct((nb, 1, LANE), F32)],
        compiler_params=_cparams(("arbitrary",)),
        name="mlstm_decode",
    )(proj.reshape(nb, 1, width), gcol.reshape(nb, 1, LANE), bias,
      mh_norm_w.reshape(1, nh * dv), c0, n0, m0.reshape(nb, 1, nh))


def _lambda_full(lq1, lk1, lq2, lk2, lam_init):
    a = jnp.exp(jnp.sum(lq1 * lk1, axis=-1, keepdims=True))
    b = jnp.exp(jnp.sum(lq2 * lk2, axis=-1, keepdims=True))
    return a - b + lam_init


def _attn_prompt_kernel(q_ref, k_ref, v_ref, lq1_ref, lk1_ref, lq2_ref, lk2_ref, sw_ref,
                        o_ref, acc_ref, ml_ref, *, dh, lam_init, tk):
    qi = pl.program_id(2)
    tq = q_ref.shape[0]
    scale = dh ** -0.5
    q = q_ref[...]
    qs = (q[:, :dh], q[:, dh:])
    acc_ref[...] = jnp.zeros_like(acc_ref)
    ml_ref[0] = jnp.full(ml_ref.shape[1:], -jnp.inf, F32)
    ml_ref[1] = jnp.zeros(ml_ref.shape[1:], F32)
    ml_ref[2] = jnp.full(ml_ref.shape[1:], -jnp.inf, F32)
    ml_ref[3] = jnp.zeros(ml_ref.shape[1:], F32)
    row = lax.broadcasted_iota(jnp.int32, (tq, tk), 0) + qi * tq
    col0 = lax.broadcasted_iota(jnp.int32, (tq, tk), 1)
    n_kv = (qi * tq + tq + tk - 1) // tk

    def body(j, carry):
        start = pl.multiple_of(j * tk, tk)
        kb = k_ref[pl.ds(start, tk), :].astype(BF16)
        vb = v_ref[pl.ds(start, tk), :].astype(BF16)
        mask = (col0 + j * tk) <= row
        for c in range(2):
            s = _dot_nt(qs[c], kb[:, c * dh:(c + 1) * dh]) * scale
            s = jnp.where(mask, s, -jnp.inf)
            m_old = ml_ref[2 * c][:, 0:1]
            l_old = ml_ref[2 * c + 1][:, 0:1]
            m_new = jnp.maximum(m_old, jnp.max(s, axis=-1, keepdims=True))
            alpha = jnp.exp(m_old - m_new)
            p = jnp.exp(s - m_new)
            l_new = alpha * l_old + jnp.sum(p, axis=-1, keepdims=True)
            acc_ref[c] = alpha * acc_ref[c] + _dot(p.astype(BF16), vb)
            ml_ref[2 * c] = jnp.broadcast_to(m_new, ml_ref.shape[1:])
            ml_ref[2 * c + 1] = jnp.broadcast_to(l_new, ml_ref.shape[1:])
        return carry

    lax.fori_loop(0, n_kv, body, 0)
    lam = _lambda_full(lq1_ref[...], lk1_ref[...], lq2_ref[...], lk2_ref[...], lam_init)
    o = acc_ref[0] / ml_ref[1][:, 0:1] - lam * (acc_ref[1] / ml_ref[3][:, 0:1])
    y = o * lax.rsqrt(jnp.mean(o * o, axis=-1, keepdims=True) + EPS) * sw_ref[...]
    o_ref[...] = (y * (1.0 - lam_init)).astype(o_ref.dtype)


def _attn_prompt(q, k, v, lams, subln_w, batch, seq, dh, dv, lam_init, tq=256, tk=256):
    tq = min(tq, seq)
    tk = min(tk, seq)
    nq = seq // tq
    m = batch * seq
    kern = functools.partial(_attn_prompt_kernel, dh=dh, lam_init=lam_init, tk=tk)
    lam_spec = pl.BlockSpec((1, dh), lambda b, h, i: (0, 0))
    return pl.pallas_call(
        kern,
        grid=(batch, A_HEADS, nq),
        in_specs=[pl.BlockSpec((tq, 2 * dh), lambda b, h, i: (b * nq + i, h)),
                  pl.BlockSpec((seq, 2 * dh), lambda b, h, i: (b, h)),
                  pl.BlockSpec((seq, dv), lambda b, h, i: (b, h)),
                  lam_spec, lam_spec, lam_spec, lam_spec,
                  pl.BlockSpec((1, dv), lambda b, h, i: (0, 0))],
        out_specs=pl.BlockSpec((tq, dv), lambda b, h, i: (b * nq + i, h)),
        out_shape=jax.ShapeDtypeStruct((m, A_HEADS * dv), BF16),
        scratch_shapes=[pltpu.VMEM((2, tq, dv), F32),
                        pltpu.VMEM((4, tq, LANE), F32)],
        compiler_params=_cparams(("arbitrary", "arbitrary", "arbitrary")),
        name="attn_prompt",
    )(q, k, v, *[l.reshape(1, dh) for l in lams], subln_w.reshape(1, dv))


def _attn_decode_kernel(pt_ref, q_ref, kn_ref, vn_ref, kp_ref, vp_ref,
                        lq1_ref, lk1_ref, lq2_ref, lk2_ref, sw_ref,
                        o_ref, qrows_ref, acc_ref, ml_ref, *, dh, dv, lam_init, n_pages):
    p_idx = pl.program_id(1)
    nh = A_HEADS
    nr = 2 * nh
    width = nh * 2 * dh
    scale = dh ** -0.5

    @pl.when(p_idx == 0)
    def _():
        r = lax.broadcasted_iota(jnp.int32, (nr, width), 0)
        cblk = lax.broadcasted_iota(jnp.int32, (nr, width), 1) // dh
        want = (r % nh) * 2 + r // nh
        qb = jnp.broadcast_to(q_ref[0].astype(F32), (nr, width))
        qrows_ref[...] = jnp.where(cblk == want, qb, 0.0)
        acc_ref[...] = jnp.zeros_like(acc_ref)
        ml_ref[0] = jnp.full(ml_ref.shape[1:], -jnp.inf, F32)
        ml_ref[1] = jnp.zeros(ml_ref.shape[1:], F32)

    rsel = lax.broadcasted_iota(jnp.int32, (nr, dv), 0) % nh

    def diag_blocks(res):
        out = jnp.zeros((nr, dv), F32)
        for h in range(nh):
            out = jnp.where(rsel == h, res[:, h * dv:(h + 1) * dv], out)
        return out

    def update(s, pv_fn):
        m_old = ml_ref[0][:, 0:1]
        l_old = ml_ref[1][:, 0:1]
        m_new = jnp.maximum(m_old, jnp.max(s, axis=-1, keepdims=True))
        alpha = jnp.exp(m_old - m_new)
        p = jnp.exp(s - m_new)
        l_new = alpha * l_old + jnp.sum(p, axis=-1, keepdims=True)
        acc_ref[...] = alpha * acc_ref[...] + diag_blocks(pv_fn(p))
        ml_ref[0] = jnp.broadcast_to(m_new, ml_ref.shape[1:])
        ml_ref[1] = jnp.broadcast_to(l_new, ml_ref.shape[1:])

    qrows = qrows_ref[...]
    kpage = kp_ref[0].astype(BF16)
    vpage = vp_ref[0].astype(BF16)
    s_page = _dot_nt(qrows.astype(BF16), kpage) * scale
    update(s_page, lambda p: _dot(p.astype(BF16), vpage))

    @pl.when(p_idx == n_pages - 1)
    def _():
        s_new = jnp.sum(qrows * kn_ref[0], axis=-1, keepdims=True) * scale
        vn = vn_ref[0]
        update(s_new, lambda p: p * jnp.broadcast_to(vn, (nr, nh * dv)))
        lam = _lambda_full(lq1_ref[...], lk1_ref[...], lq2_ref[...], lk2_ref[...], lam_init)
        o_n = acc_ref[...] / ml_ref[1][:, 0:1]
        o = o_n[:nh] - lam * o_n[nh:]
        y = o * lax.rsqrt(jnp.mean(o * o, axis=-1, keepdims=True) + EPS) * sw_ref[...]
        o_ref[0] = (y * (1.0 - lam_init)).astype(o_ref.dtype)


def _attn_decode(page_table, q, k_new, v_new, cache_k, cache_v, lams, subln_w, dh, dv, lam_init):
    nb, n_pages = page_table.shape
    page = cache_k.shape[1]
    width = cache_k.shape[2]
    nh = A_HEADS
    kern = functools.partial(_attn_decode_kernel, dh=dh, dv=dv, lam_init=lam_init, n_pages=n_pages)
    lam_spec = pl.BlockSpec((1, dh), lambda b, p, pt: (0, 0))
    grid_spec = pltpu.PrefetchScalarGridSpec(
        num_scalar_prefetch=1,
        grid=(nb, n_pages),
        in_specs=[pl.BlockSpec((1, 1, width), lambda b, p, pt: (b, 0, 0)),
                  pl.BlockSpec((1, 1, width), lambda b, p, pt: (b, 0, 0)),
                  pl.BlockSpec((1, 1, nh * dv), lambda b, p, pt: (b, 0, 0)),
                  pl.BlockSpec((1, page, width), lambda b, p, pt: (pt[b, p], 0, 0)),
                  pl.BlockSpec((1, page, nh * dv), lambda b, p, pt: (pt[b, p], 0, 0)),
                  lam_spec, lam_spec, lam_spec, lam_spec,
                  pl.BlockSpec((1, dv), lambda b, p, pt: (0, 0))],
        out_specs=pl.BlockSpec((1, nh, dv), lambda b, p, pt: (b, 0, 0)),
        scratch_shapes=[pltpu.VMEM((2 * nh, width), F32),
                        pltpu.VMEM((2 * nh, dv), F32),
                        pltpu.VMEM((2, 2 * nh, LANE), F32)],
    )
    out = pl.pallas_call(
        kern,
        grid_spec=grid_spec,
        out_shape=jax.ShapeDtypeStruct((nb, nh, dv), BF16),
        compiler_params=_cparams(("arbitrary", "arbitrary")),
        name="attn_decode",
    )(page_table, q.reshape(nb, 1, width), k_new.reshape(nb, 1, width), v_new.reshape(nb, 1, nh * dv),
      cache_k, cache_v, *[l.reshape(1, dh) for l in lams], subln_w.reshape(1, dv))
    return out.reshape(nb, nh * dv)


def _split_w_in(w_in, d, mq, mv, aq, av):
    o1 = 2 * mq + 2 * mv
    o2 = o1 + 2 * M_HEADS
    w_m = w_in[:, :o1].astype(BF16)
    w_g = w_in[:, o1:o2]
    w_aq = w_in[:, o2:o2 + aq].astype(BF16)
    w_ak = w_in[:, o2 + aq:o2 + 2 * aq].astype(BF16)
    w_av = w_in[:, o2 + 2 * aq:o2 + 2 * aq + av].astype(BF16)
    return w_m, w_g, w_aq, w_ak, w_av


def _pick(m, pref):
    return pref if m % pref == 0 else m


def _group(x, mods, rows_per_mod, weights, mixer):
    (norm1_w, w_m, w_g, w_aq, w_ak, w_av, q_norm_w, k_norm_w, w_out_m, w_out_a,
     norm2_w, w_up, w_down) = weights
    sh1, sc1, g1, sh2, sc2, g2 = mods
    m, d = x.shape
    unit = m if rows_per_mod == 1 else rows_per_mod
    tm = _pick(unit, 1024)
    tr = _pick(unit, 256)
    h = _norm_mod(x, norm1_w, sc1, sh1, rows_per_mod, tr)
    proj_m = _mm([(h, w_m)], tm=tm, tn=512, name="in_mlstm")
    gcol, grow = _gate_proj(h, w_g, _pick(m, 512))
    qa = _mm([(h, w_aq)], epilogue="qknorm", normw=q_norm_w, out_dtype=BF16, tm=tm, tn=512, name="in_aq")
    ka = _mm([(h, w_ak)], epilogue="qknorm", normw=k_norm_w, tm=tm, tn=512, name="in_ak")
    va = _mm([(h, w_av)], tm=tm, tn=512, name="in_av")
    hm, oa, state = mixer(proj_m, gcol, grow, qa, ka, va)
    x1 = _mm([(hm, w_out_m), (oa, w_out_a)], epilogue="resid", resid=x, gate=g1,
             rows_per_gate=rows_per_mod, tm=tm, tn=512, name="out_proj")
    h2 = _norm_mod(x1, norm2_w, sc2, sh2, rows_per_mod, tr)
    u = _mm([(h2, w_up)], epilogue="relu2", out_dtype=BF16, tm=tm, tn=512, name="ffn_up")
    y = _mm([(u, w_down)], epilogue="resid", resid=x1, gate=g2, rows_per_gate=rows_per_mod,
            tm=tm, tn=_pick(d, 1024), tk=2048, name="ffn_down")
    return y, ka, va, state


def kernel(x_prompt, x_sample, cache_k, cache_v, state_C, state_n, state_m, page_table,
           c_prompt, c_sample, w_ada, b_ada, norm1_w, w_in, b_igate, b_fgate, mh_norm_w,
           q_norm_w, k_norm_w, lambda_q1, lambda_k1, lambda_q2, lambda_k2, subln_w,
           w_out, norm2_w, w_up, w_down):
    depth = w_ada.shape[0]
    batch, seq, d = x_prompt.shape
    nb = x_sample.shape[0]
    dk = state_C.shape[3]
    dv = state_C.shape[4]
    dh = cache_k.shape[-1]
    adv = cache_v.shape[-1]
    mq, mv = M_HEADS * dk, M_HEADS * dv
    aq, av = A_HEADS * 2 * dh, A_HEADS * adv
    n_pool, page = cache_k.shape[1], cache_k.shape[2]

    yp = x_prompt.reshape(batch * seq, d)
    ys = x_sample.reshape(nb, d)
    outs = [[] for _ in range(10)]
    for l in range(depth):
        lam_init = 0.8 - 0.6 * math.exp(-0.3 * l)
        lams = (lambda_q1[l], lambda_k1[l], lambda_q2[l], lambda_k2[l])

        c_all = jnp.concatenate([c_prompt, c_sample], axis=0)
        pad = (-c_all.shape[0]) % 8
        mod = _ada(jnp.pad(c_all, ((0, pad), (0, 0))), w_ada[l], b_ada[l])
        mods_p = tuple(mod[:batch, i * d:(i + 1) * d] for i in range(N_MOD))
        mods_s = tuple(mod[batch:batch + nb, i * d:(i + 1) * d] for i in range(N_MOD))

        w_m, w_g, w_aq, w_ak, w_av = _split_w_in(w_in[l], d, mq, mv, aq, av)
        w_out_b = w_out[l].astype(BF16)
        weights = (norm1_w[l], w_m, w_g, w_aq, w_ak, w_av, q_norm_w[l], k_norm_w[l],
                   w_out_b[:mv], w_out_b[mv:], norm2_w[l], w_up[l].astype(BF16), w_down[l].astype(BF16))

        def prompt_mixer(proj_m, gcol, grow, qa, ka, va):
            hm, c_new, n_new, m_new = _mlstm_prompt(proj_m, gcol, grow, b_igate[l], b_fgate[l],
                                                    mh_norm_w[l], batch, seq, dk, dv)
            oa = _attn_prompt(qa, ka, va, lams, subln_w[l], batch, seq, dh, adv, lam_init)
            return hm, oa, (c_new, n_new[..., 0], m_new[:, 0, :M_HEADS])

        def sample_mixer(proj_m, gcol, grow, qa, ka, va):
            hm, c_new, n_new, m_new = _mlstm_decode(proj_m, gcol, b_igate[l], b_fgate[l], mh_norm_w[l],
                                                    state_C[l], state_n[l], state_m[l], dk, dv)
            oa = _attn_decode(page_table, qa, ka, va,
                              cache_k[l].reshape(n_pool, page, aq), cache_v[l].reshape(n_pool, page, av),
                              lams, subln_w[l], dh, adv, lam_init)
            return hm.reshape(nb, mv), oa, (c_new, n_new, m_new[:, 0, :M_HEADS])

        yp, kp, vp, (cp, np_, mp) = _group(yp, mods_p, seq, weights, prompt_mixer)
        ys, ks, vs, (cs, ns, ms) = _group(ys, mods_s, 1, weights, sample_mixer)
        for lst, val in zip(outs, (kp.reshape(batch, seq, A_HEADS, 2, dh), vp.reshape(batch, seq, A_HEADS, adv),
                                   cp, np_, mp,
                                   ks.reshape(nb, 1, A_HEADS, 2, dh), vs.reshape(nb, 1, A_HEADS, adv),
                                   cs, ns, ms)):
            lst.append(val)
    stacked = [jnp.stack(o) for o in outs]
    return (yp.reshape(batch, seq, d), ys.reshape(nb, 1, d), *stacked)
```

```python
import functools
import math

import jax
import jax.numpy as jnp
from jax import lax
from jax.experimental import pallas as pl
from jax.experimental.pallas import tpu as pltpu

F32 = jnp.float32
BF16 = jnp.bfloat16
EPS = 1e-6
LOG2E = math.log2(math.e)
LANE = 128
VMEM_LIMIT = 56 * 1024 * 1024

M_HEADS = 4
A_HEADS = 8
N_MOD = 6
M_CHUNK = 256
ATTN_TQ = 512
DECODE_PAGES = 8


def _cparams(sem):
    return pltpu.CompilerParams(dimension_semantics=sem, vmem_limit_bytes=VMEM_LIMIT)


def _dot(a, b):
    return jnp.dot(a, b, preferred_element_type=F32)


def _dot_nt(a, b):
    return lax.dot_general(a, b, (((1,), (1,)), ((), ())), preferred_element_type=F32)


def _dot_tn(a, b):
    return lax.dot_general(a, b, (((0,), (0,)), ((), ())), preferred_element_type=F32)


def _log_sigmoid(x):
    return jnp.minimum(x, 0.0) - jnp.log(1.0 + jnp.exp(-jnp.abs(x)))


def _sigmoid(x):
    return 1.0 / (1.0 + jnp.exp(-x))


def _ada_kernel(c_ref, w_ref, b_ref, o_ref):
    c = c_ref[...]
    a = (c * _sigmoid(c)).astype(BF16)
    o_ref[...] = _dot(a, w_ref[...].astype(BF16)) + b_ref[...]


def _ada(c, w, b, tn=512):
    m, k = c.shape
    n = w.shape[1]
    return pl.pallas_call(
        _ada_kernel,
        grid=(n // tn,),
        in_specs=[pl.BlockSpec((m, k), lambda j: (0, 0)),
                  pl.BlockSpec((k, tn), lambda j: (0, j)),
                  pl.BlockSpec((1, tn), lambda j: (0, j))],
        out_specs=pl.BlockSpec((m, tn), lambda j: (0, j)),
        out_shape=jax.ShapeDtypeStruct((m, n), F32),
        compiler_params=_cparams(("arbitrary",)),
        name="ada",
    )(c, w, b.reshape(1, n))


def _norm_mod_kernel(x_ref, w_ref, sc_ref, sh_ref, o_ref):
    x = x_ref[...]
    sc = sc_ref[...].reshape(-1, x.shape[-1])
    sh = sh_ref[...].reshape(-1, x.shape[-1])
    y = x * lax.rsqrt(jnp.mean(x * x, axis=-1, keepdims=True) + EPS) * w_ref[...]
    o_ref[...] = (y * (1.0 + sc) + sh).astype(o_ref.dtype)


def _norm_mod(x, w, sc, sh, rows_per_mod, tr):
    m, d = x.shape
    if rows_per_mod == 1:
        mod_spec = pl.BlockSpec((tr, d), lambda i: (i, 0))
    else:
        per = rows_per_mod // tr
        sc = sc.reshape(-1, 1, d)
        sh = sh.reshape(-1, 1, d)
        mod_spec = pl.BlockSpec((1, 1, d), lambda i: (i // per, 0, 0))
    return pl.pallas_call(
        _norm_mod_kernel,
        grid=(m // tr,),
        in_specs=[pl.BlockSpec((tr, d), lambda i: (i, 0)),
                  pl.BlockSpec((1, d), lambda i: (0, 0)),
                  mod_spec, mod_spec],
        out_specs=pl.BlockSpec((tr, d), lambda i: (i, 0)),
        out_shape=jax.ShapeDtypeStruct((m, d), BF16),
        compiler_params=_cparams(("arbitrary",)),
        name="norm_mod",
    )(x, w.reshape(1, d), sc, sh)


def _mm_kernel(*refs, n_pairs, epilogue, nk, post_scale):
    ab = refs[:2 * n_pairs]
    rest = refs[2 * n_pairs:]
    if nk > 1:
        acc_ref = rest[-1]
        rest = rest[:-1]
    o_ref = rest[-1]
    extras = rest[:-1]

    def partial_product():
        r = _dot(ab[0][...], ab[1][...])
        for p in range(1, n_pairs):
            r = r + _dot(ab[2 * p][...], ab[2 * p + 1][...])
        return r

    def finish(acc):
        if epilogue == "plain":
            o_ref[...] = acc.astype(o_ref.dtype)
        elif epilogue == "relu2":
            r = jnp.maximum(acc, 0.0)
            o_ref[...] = (r * r).astype(o_ref.dtype)
        elif epilogue == "qknorm":
            w = extras[0][...]
            for j in range(acc.shape[-1] // LANE):
                blk = acc[:, j * LANE:(j + 1) * LANE]
                y = blk * lax.rsqrt(jnp.mean(blk * blk, axis=-1, keepdims=True) + EPS) * w
                if post_scale != 1.0:
                    y = y * post_scale
                o_ref[:, j * LANE:(j + 1) * LANE] = y.astype(o_ref.dtype)
        elif epilogue == "resid":
            x = extras[0][...]
            g = extras[1][...].reshape(-1, acc.shape[-1])
            o_ref[...] = (x + g * acc).astype(o_ref.dtype)
        else:
            raise ValueError(epilogue)

    if nk == 1:
        finish(partial_product())
    else:
        k = pl.program_id(2)

        @pl.when(k == 0)
        def _():
            acc_ref[...] = partial_product()

        @pl.when(k > 0)
        def _():
            acc_ref[...] += partial_product()

        @pl.when(k == nk - 1)
        def _():
            finish(acc_ref[...])


def _mm(pairs, *, epilogue="plain", out_dtype=F32, tm, tn, tk=None, n_out=None,
        normw=None, post_scale=1.0, resid=None, gate=None, rows_per_gate=1, name="mm"):
    m, kdim = pairs[0][0].shape
    n = pairs[0][1].shape[1] if n_out is None else n_out
    tk = kdim if tk is None else tk
    nk = kdim // tk
    assert m % tm == 0 and n % tn == 0 and kdim % tk == 0
    grid = (m // tm, n // tn, nk)
    in_specs, args = [], []
    for a, w in pairs:
        assert a.shape == (m, kdim) and w.shape[0] == kdim and w.shape[1] >= n
        in_specs += [pl.BlockSpec((tm, tk), lambda i, j, k: (i, k)),
                     pl.BlockSpec((tk, tn), lambda i, j, k: (k, j))]
        args += [a, w]
    if epilogue == "qknorm":
        in_specs.append(pl.BlockSpec((1, LANE), lambda i, j, k: (0, 0)))
        args.append(normw.reshape(1, LANE))
    if epilogue == "resid":
        in_specs.append(pl.BlockSpec((tm, tn), lambda i, j, k: (i, j)))
        args.append(resid)
        if rows_per_gate == 1:
            in_specs.append(pl.BlockSpec((tm, tn), lambda i, j, k: (i, j)))
            args.append(gate)
        else:
            per = rows_per_gate // tm
            in_specs.append(pl.BlockSpec((1, 1, tn), lambda i, j, k: (i // per, 0, j)))
            args.append(gate.reshape(-1, 1, n))
    scratch = [pltpu.VMEM((tm, tn), F32)] if nk > 1 else []
    return pl.pallas_call(
        functools.partial(_mm_kernel, n_pairs=len(pairs), epilogue=epilogue, nk=nk, post_scale=post_scale),
        grid=grid,
        in_specs=in_specs,
        out_specs=pl.BlockSpec((tm, tn), lambda i, j, k: (i, j)),
        out_shape=jax.ShapeDtypeStruct((m, n), out_dtype),
        scratch_shapes=scratch,
        compiler_params=_cparams(("arbitrary", "arbitrary", "arbitrary")),
        name=name,
    )(*args)


def _gate_kernel(h_ref, wc_ref, wr_ref, oc_ref, or_ref):
    h = h_ref[...]
    oc_ref[...] = _dot(h, wc_ref[...])
    or_ref[...] = _dot_nt(wr_ref[...], h)


def _gate_proj(h, w_gate_cols, tm):
    m, k = h.shape
    wc = jnp.pad(w_gate_cols, ((0, 0), (0, LANE - 8))).astype(BF16)
    wr = jnp.pad(w_gate_cols.T, ((0, 8), (0, 0))).astype(BF16)
    return pl.pallas_call(
        _gate_kernel,
        grid=(m // tm,),
        in_specs=[pl.BlockSpec((tm, k), lambda i: (i, 0)),
                  pl.BlockSpec((k, LANE), lambda i: (0, 0)),
                  pl.BlockSpec((16, k), lambda i: (0, 0))],
        out_specs=[pl.BlockSpec((tm, LANE), lambda i: (i, 0)),
                   pl.BlockSpec((16, tm), lambda i: (0, i))],
        out_shape=[jax.ShapeDtypeStruct((m, LANE), F32),
                   jax.ShapeDtypeStruct((16, m), F32)],
        compiler_params=_cparams(("arbitrary",)),
        name="gate_proj",
    )(h, wc, wr)


def _mlstm_prompt_kernel(proj_ref, gc_ref, gr_ref, bc_ref, br_ref, nw_ref,
                         hm_ref, c_out_ref, n_out_ref, m_out_ref,
                         cext_ref, m_ref, *, dk, dv, nc):
    c_idx = pl.program_id(1)
    L = gc_ref.shape[0]
    nh = M_HEADS
    scale = dk ** -0.5

    @pl.when(c_idx == 0)
    def _():
        cext_ref[...] = jnp.zeros_like(cext_ref)
        m_ref[...] = jnp.zeros_like(m_ref)

    row = lax.broadcasted_iota(jnp.int32, (L, L), 0)
    col = lax.broadcasted_iota(jnp.int32, (L, L), 1)
    causal = row >= col
    tril = causal.astype(F32)

    gcol = gc_ref[...] + bc_ref[...]
    grow = gr_ref[...] + br_ref[...]
    lf_col = _log_sigmoid(gcol)
    lf_row = _log_sigmoid(grow)
    bt_col_all = jnp.dot(tril, lf_col, preferred_element_type=F32, precision=lax.Precision.HIGHEST)
    bt_row_all = lax.dot_general(lf_row, tril, (((1,), (1,)), ((), ())),
                                 preferred_element_type=F32, precision=lax.Precision.HIGHEST)

    lane = lax.broadcasted_iota(jnp.int32, (1, LANE), 1)
    m_new_row = jnp.zeros((1, LANE), F32)
    ones_pad = jnp.ones((L, LANE), BF16)

    for h in range(nh):
        it_c = gcol[:, h:h + 1]
        bt_c = bt_col_all[:, nh + h:nh + h + 1]
        it_r = grow[h:h + 1, :]
        bt_r = bt_row_all[nh + h:nh + h + 1, :]
        m0 = m_ref[h][0:1, 0:1]

        log_d = jnp.where(causal, bt_c - bt_r + it_r, -jnp.inf)
        m_inter = bt_c + m0
        m = jnp.maximum(m_inter, jnp.max(log_d, axis=-1, keepdims=True))
        d_m = jnp.exp(log_d - m)
        g = jnp.exp(m_inter - m)

        q = proj_ref[:, h * dk:(h + 1) * dk].astype(BF16)
        k_f = proj_ref[:, nh * dk + h * dk:nh * dk + (h + 1) * dk] * scale
        v_off = 2 * nh * dk
        v = proj_ref[:, v_off + h * dv:v_off + (h + 1) * dv].astype(BF16)
        vext = jnp.concatenate([v, ones_pad], axis=-1)
        o_off = v_off + nh * dv
        o_gate = proj_ref[:, o_off + h * dv:o_off + (h + 1) * dv]

        s = _dot_nt(q, k_f.astype(BF16)) * d_m
        cext = cext_ref[h]
        numext = _dot(s.astype(BF16), vext) + _dot(q, cext.astype(BF16)) * g
        num = numext[:, :dv]
        den = numext[:, dv:dv + 1]
        den = jnp.maximum(jnp.abs(den), jnp.exp(-m))
        hh = num / den
        y = hh * lax.rsqrt(jnp.mean(hh * hh, axis=-1, keepdims=True) + EPS) * nw_ref[:, h * dv:(h + 1) * dv]
        hm_ref[:, h * dv:(h + 1) * dv] = (_sigmoid(o_gate) * y).astype(hm_ref.dtype)

        m_last = m[L - 1:L, :]
        bt_last = bt_c[L - 1:L, :]
        w_c = jnp.exp(bt_last - bt_c + it_c - m_last)
        g_last = jnp.exp(bt_last + m0 - m_last)
        kw = (k_f * w_c).astype(BF16)
        cext_ref[h] = g_last * cext + _dot_tn(kw, vext)
        m_ref[h] = jnp.broadcast_to(m_last, m_ref.shape[1:])
        m_new_row = jnp.where(lane == h, m_last, m_new_row)

    @pl.when(c_idx == nc - 1)
    def _():
        for h in range(nh):
            c_out_ref[0, h] = cext_ref[h][:, :dv]
            n_out_ref[0, h] = cext_ref[h][:, dv:]
        m_out_ref[0] = m_new_row


def _mlstm_prompt(proj, gcol, grow, b_ig, b_fg, mh_norm_w, batch, seq, dk, dv):
    nh = M_HEADS
    L = min(M_CHUNK, seq)
    nc = seq // L
    m = batch * seq
    width = proj.shape[1]
    bias = jnp.concatenate([b_ig, b_fg]).astype(F32)
    bias_cols = jnp.pad(bias, (0, LANE - 2 * nh)).reshape(1, LANE)
    bias_rows = jnp.pad(bias, (0, 16 - 2 * nh)).reshape(16, 1)
    kern = functools.partial(_mlstm_prompt_kernel, dk=dk, dv=dv, nc=nc)
    return pl.pallas_call(
        kern,
        grid=(batch, nc),
        in_specs=[pl.BlockSpec((L, width), lambda b, c: (b * nc + c, 0)),
                  pl.BlockSpec((L, LANE), lambda b, c: (b * nc + c, 0)),
                  pl.BlockSpec((16, L), lambda b, c: (0, b * nc + c)),
                  pl.BlockSpec((1, LANE), lambda b, c: (0, 0)),
                  pl.BlockSpec((16, 1), lambda b, c: (0, 0)),
                  pl.BlockSpec((1, nh * dv), lambda b, c: (0, 0))],
        out_specs=[pl.BlockSpec((L, nh * dv), lambda b, c: (b * nc + c, 0)),
                   pl.BlockSpec((1, nh, dk, dv), lambda b, c: (b, 0, 0, 0)),
                   pl.BlockSpec((1, nh, dk, LANE), lambda b, c: (b, 0, 0, 0)),
                   pl.BlockSpec((1, 1, LANE), lambda b, c: (b, 0, 0))],
        out_shape=[jax.ShapeDtypeStruct((m, nh * dv), BF16),
                   jax.ShapeDtypeStruct((batch, nh, dk, dv), F32),
                   jax.ShapeDtypeStruct((batch, nh, dk, LANE), F32),
                   jax.ShapeDtypeStruct((batch, 1, LANE), F32)],
        scratch_shapes=[pltpu.VMEM((nh, dk, dv + LANE), F32),
                        pltpu.VMEM((nh, 8, LANE), F32)],
        compiler_params=_cparams(("arbitrary", "arbitrary")),
        name="mlstm_prompt",
    )(proj, gcol, grow, bias_cols, bias_rows, mh_norm_w.reshape(1, nh * dv))


def _mlstm_decode_kernel(proj_ref, g_ref, bias_ref, nw_ref, c0_ref, n0_ref, m0_ref,
                         hm_ref, c_out_ref, n_out_ref, m_out_ref, *, dk, dv):
    nh = M_HEADS
    scale = dk ** -0.5
    gates = g_ref[0] + bias_ref[...]
    lf_all = _log_sigmoid(gates)
    m0_all = m0_ref[0]
    lane = lax.broadcasted_iota(jnp.int32, (1, LANE), 1)
    eye = (lax.broadcasted_iota(jnp.int32, (dk, dk), 0)
           == lax.broadcasted_iota(jnp.int32, (dk, dk), 1))
    m_new_row = jnp.zeros((1, LANE), F32)

    def to_col(r):
        return jnp.sum(jnp.where(eye, jnp.broadcast_to(r, (dk, dk)), 0.0), axis=1, keepdims=True)

    v_off = 2 * nh * dk
    o_off = v_off + nh * dv
    for h in range(nh):
        it = gates[:, h:h + 1]
        lf = lf_all[:, nh + h:nh + h + 1]
        m0 = m0_all[:, h:h + 1]
        q = proj_ref[0, :, h * dk:(h + 1) * dk]
        k = proj_ref[0, :, nh * dk + h * dk:nh * dk + (h + 1) * dk] * scale
        v = proj_ref[0, :, v_off + h * dv:v_off + (h + 1) * dv]
        o_gate = proj_ref[0, :, o_off + h * dv:o_off + (h + 1) * dv]
        c0 = c0_ref[0, h]
        n0 = n0_ref[0, h:h + 1, :]

        m_inter = lf + m0
        m = jnp.maximum(m_inter, it)
        d_m = jnp.exp(it - m)
        g = jnp.exp(m_inter - m)
        s = jnp.sum(q * k, axis=-1, keepdims=True) * d_m
        q_col = to_col(q)
        qc = jnp.sum(q_col * c0, axis=0, keepdims=True)
        num = s * v + qc * g
        den = s + jnp.sum(q * n0, axis=-1, keepdims=True) * g
        den = jnp.maximum(jnp.abs(den), jnp.exp(-m))
        hh = num / den
        y = hh * lax.rsqrt(jnp.mean(hh * hh, axis=-1, keepdims=True) + EPS) * nw_ref[:, h * dv:(h + 1) * dv]
        hm_ref[0, :, h * dv:(h + 1) * dv] = (_sigmoid(o_gate) * y).astype(hm_ref.dtype)

        kw = k * d_m
        c_out_ref[0, h] = g * c0 + to_col(kw) * v
        n_out_ref[0, h:h + 1, :] = g * n0 + kw
        m_new_row = jnp.where(lane == h, m, m_new_row)
    m_out_ref[0] = m_new_row


def _mlstm_decode(proj, gcol, b_ig, b_fg, mh_norm_w, c_all, n_all, m0, layer, dk, dv):
    nh = M_HEADS
    nb, width = proj.shape
    bias = jnp.pad(jnp.concatenate([b_ig, b_fg]).astype(F32), (0, LANE - 2 * nh)).reshape(1, LANE)
    kern = functools.partial(_mlstm_decode_kernel, dk=dk, dv=dv)
    c_flat = c_all.reshape(-1, nh, dk, dv)
    n_flat = n_all.reshape(-1, nh, dk)
    return pl.pallas_call(
        kern,
        grid=(nb,),
        in_specs=[pl.BlockSpec((1, 1, width), lambda b: (b, 0, 0)),
                  pl.BlockSpec((1, 1, LANE), lambda b: (b, 0, 0)),
                  pl.BlockSpec((1, LANE), lambda b: (0, 0)),
                  pl.BlockSpec((1, nh * dv), lambda b: (0, 0)),
                  pl.BlockSpec((1, nh, dk, dv), lambda b: (layer * nb + b, 0, 0, 0)),
                  pl.BlockSpec((1, nh, dk), lambda b: (layer * nb + b, 0, 0)),
                  pl.BlockSpec((1, 1, nh), lambda b: (b, 0, 0))],
        out_specs=[pl.BlockSpec((1, 1, nh * dv), lambda b: (b, 0, 0)),
                   pl.BlockSpec((1, nh, dk, dv), lambda b: (b, 0, 0, 0)),
                   pl.BlockSpec((1, nh, dk), lambda b: (b, 0, 0)),
                   pl.BlockSpec((1, 1, LANE), lambda b: (b, 0, 0))],
        out_shape=[jax.ShapeDtypeStruct((nb, 1, nh * dv), BF16),
                   jax.ShapeDtypeStruct((nb, nh, dk, dv), F32),
                   jax.ShapeDtypeStruct((nb, nh, dk), F32),
                   jax.ShapeDtypeStruct((nb, 1, LANE), F32)],
        compiler_params=_cparams(("arbitrary",)),
        name="mlstm_decode",
    )(proj.reshape(nb, 1, width), gcol.reshape(nb, 1, LANE), bias,
      mh_norm_w.reshape(1, nh * dv), c_flat, n_flat, m0.reshape(nb, 1, nh))


def _lambda_full(lq1, lk1, lq2, lk2, lam_init):
    a = jnp.exp(jnp.sum(lq1 * lk1, axis=-1, keepdims=True))
    b = jnp.exp(jnp.sum(lq2 * lk2, axis=-1, keepdims=True))
    return a - b + lam_init


def _lane_fold(x, op):
    out = x[:, :LANE]
    for i in range(1, x.shape[-1] // LANE):
        out = op(out, x[:, i * LANE:(i + 1) * LANE])
    return out


def _attn_prompt_kernel(q_ref, k_ref, v_ref, lq1_ref, lk1_ref, lq2_ref, lk2_ref, sw_ref,
                        o_ref, kb_ref, vb_ref, s_ref, p_ref, *, dh, lam_init, tq):
    seq = q_ref.shape[0]
    kb_ref[...] = k_ref[...].astype(BF16)
    vb_ref[...] = v_ref[...].astype(BF16)
    lam = _lambda_full(lq1_ref[...], lk1_ref[...], lq2_ref[...], lk2_ref[...], lam_init)
    causal = (lax.broadcasted_iota(jnp.int32, (tq, tq), 0)
              >= lax.broadcasted_iota(jnp.int32, (tq, tq), 1))
    for qi in range(seq // tq):
        kv_len = (qi + 1) * tq
        outs = []
        for c in range(2):
            q_c = q_ref[qi * tq:(qi + 1) * tq, c * dh:(c + 1) * dh]
            mx = None
            for j in range(qi + 1):
                s = _dot_nt(q_c, kb_ref[j * tq:(j + 1) * tq, c * dh:(c + 1) * dh])
                if j == qi:
                    s = jnp.where(causal, s, -jnp.inf)
                s_ref[:, j * tq:(j + 1) * tq] = s
                part = _lane_fold(s, jnp.maximum)
                mx = part if mx is None else jnp.maximum(mx, part)
            m = jnp.max(mx, axis=-1, keepdims=True)
            ls = None
            for j in range(qi + 1):
                p = jnp.exp2(s_ref[:, j * tq:(j + 1) * tq] - m)
                part = _lane_fold(p, jnp.add)
                ls = part if ls is None else ls + part
                p_ref[:, j * tq:(j + 1) * tq] = p.astype(BF16)
            l = jnp.sum(ls, axis=-1, keepdims=True)
            outs.append(_dot(p_ref[:, :kv_len], vb_ref[:kv_len, :]) / l)
        o = outs[0] - lam * outs[1]
        y = o * lax.rsqrt(jnp.mean(o * o, axis=-1, keepdims=True) + EPS) * sw_ref[...]
        o_ref[qi * tq:(qi + 1) * tq, :] = (y * (1.0 - lam_init)).astype(o_ref.dtype)


def _attn_prompt(q, k, v, lams, subln_w, batch, seq, dh, dv, lam_init):
    tq = min(ATTN_TQ, seq)
    m = batch * seq
    kern = functools.partial(_attn_prompt_kernel, dh=dh, lam_init=lam_init, tq=tq)
    lam_spec = pl.BlockSpec((1, dh), lambda b, h: (0, 0))
    return pl.pallas_call(
        kern,
        grid=(batch, A_HEADS),
        in_specs=[pl.BlockSpec((seq, 2 * dh), lambda b, h: (b, h)),
                  pl.BlockSpec((seq, 2 * dh), lambda b, h: (b, h)),
                  pl.BlockSpec((seq, dv), lambda b, h: (b, h)),
                  lam_spec, lam_spec, lam_spec, lam_spec,
                  pl.BlockSpec((1, dv), lambda b, h: (0, 0))],
        out_specs=pl.BlockSpec((seq, dv), lambda b, h: (b, h)),
        out_shape=jax.ShapeDtypeStruct((m, A_HEADS * dv), BF16),
        scratch_shapes=[pltpu.VMEM((seq, 2 * dh), BF16),
                        pltpu.VMEM((seq, dv), BF16),
                        pltpu.VMEM((tq, seq), F32),
                        pltpu.VMEM((tq, seq), BF16)],
        compiler_params=_cparams(("arbitrary", "arbitrary")),
        name="attn_prompt",
    )(q, k, v, *[l.reshape(1, dh) for l in lams], subln_w.reshape(1, dv))


def _attn_decode_kernel(pt_ref, q_ref, kn_ref, vn_ref, *rest, dh, dv, lam_init, n_steps, n_pg, page):
    kp_refs = rest[:n_pg]
    vp_refs = rest[n_pg:2 * n_pg]
    lq1_ref, lk1_ref, lq2_ref, lk2_ref, sw_ref, o_ref, acc_ref, ml_ref = rest[2 * n_pg:]
    step = pl.program_id(1)
    nh = A_HEADS
    nr = 2 * nh
    grp = 2 * LANE
    tok_g = grp // nr
    n_grp = page * nr // grp
    qs = q_ref[0] * (dh ** -0.5 * LOG2E)

    @pl.when(step == 0)
    def _():
        acc_ref[...] = jnp.zeros_like(acc_ref)
        ml_ref[0] = jnp.full(ml_ref.shape[1:], -jnp.inf, F32)
        ml_ref[1] = jnp.zeros(ml_ref.shape[1:], F32)

    def rescale(m_blk):
        m_old = ml_ref[0][:, 0:1]
        m_new = jnp.maximum(m_old, m_blk)
        return m_new, jnp.exp2(m_old - m_new)

    def commit(m_new, alpha, l_blk, pv):
        acc_ref[...] = alpha * acc_ref[...] + pv
        ml_ref[1] = jnp.broadcast_to(alpha * ml_ref[1][:, 0:1] + l_blk, ml_ref.shape[1:])
        ml_ref[0] = jnp.broadcast_to(m_new, ml_ref.shape[1:])

    shape3 = (n_pg * n_grp, nr, grp)
    r3 = lax.broadcasted_iota(jnp.int32, shape3, 1)
    c3 = lax.broadcasted_iota(jnp.int32, shape3, 2)
    valid = (c3 % nr) == (r3 % nh) * 2 + r3 // nh
    e_row = lax.broadcasted_iota(jnp.int32, (grp, LANE), 0)
    e_col = lax.broadcasted_iota(jnp.int32, (grp, LANE), 1)
    fold = ((e_row // nr == e_col // nh) & ((e_row % nr) // 2 == e_col % nh)).astype(BF16)

    qb = qs.astype(BF16)
    pieces = []
    for pg in range(n_pg):
        s_pg = _dot_nt(qb, kp_refs[pg][...].astype(BF16))
        pieces += [s_pg[:, g * grp:(g + 1) * grp] for g in range(n_grp)]
    s3 = jnp.where(valid, jnp.concatenate(pieces, axis=0).reshape(shape3), -jnp.inf)
    m_new, alpha = rescale(jnp.max(jnp.max(s3, axis=0), axis=-1, keepdims=True))
    p3 = jnp.exp2(s3 - m_new[None])
    l_blk = jnp.sum(jnp.sum(p3, axis=0), axis=-1, keepdims=True)
    p2 = _dot(p3.reshape(n_pg * n_grp * nr, grp).astype(BF16), fold).astype(BF16)
    pv = None
    for pg in range(n_pg):
        lhs = jnp.concatenate([p2[(pg * n_grp + g) * nr:(pg * n_grp + g + 1) * nr] for g in range(n_grp)], axis=1)
        part = _dot(lhs, vp_refs[pg][...].astype(BF16))
        pv = part if pv is None else pv + part
    commit(m_new, alpha, l_blk, pv)

    @pl.when(step == n_steps - 1)
    def _():
        s_new = jnp.sum(qs * kn_ref[0], axis=-1, keepdims=True)
        m_fin, a_fin = rescale(s_new)
        p_new = jnp.exp2(s_new - m_fin)
        vn = vn_ref[0]
        commit(m_fin, a_fin, p_new, p_new * jnp.concatenate([vn, vn], axis=0))
        lam = _lambda_full(lq1_ref[...], lk1_ref[...], lq2_ref[...], lk2_ref[...], lam_init)
        o_n = acc_ref[...] / ml_ref[1][:, 0:1]
        o = o_n[:nh] - lam * o_n[nh:]
        y = o * lax.rsqrt(jnp.mean(o * o, axis=-1, keepdims=True) + EPS) * sw_ref[...]
        o_ref[0] = (y * (1.0 - lam_init)).astype(o_ref.dtype)


def _attn_decode(page_ids, q, k_new, v_new, cache_k, cache_v, lams, subln_w, dh, dv, lam_init):
    nb, n_pages = page_ids.shape
    page = cache_k.shape[-4]
    nh = A_HEADS
    nr = 2 * nh
    n_pg = math.gcd(DECODE_PAGES, n_pages)
    n_steps = n_pages // n_pg
    k2 = cache_k.reshape(-1, dh)
    v2 = cache_v.reshape(-1, dv)

    def rows_ch(x):
        return x.reshape(nb, nh, 2, dh).transpose(0, 2, 1, 3).reshape(nb, nr, dh)

    kern = functools.partial(_attn_decode_kernel, dh=dh, dv=dv, lam_init=lam_init,
                             n_steps=n_steps, n_pg=n_pg, page=page)
    lam_spec = pl.BlockSpec((1, dh), lambda b, s, pt: (0, 0))
    k_specs = [pl.BlockSpec((page * nr, dh), functools.partial(lambda b, s, pt, g: (pt[b, s * n_pg + g], 0), g=g))
               for g in range(n_pg)]
    v_specs = [pl.BlockSpec((page * nh, dv), functools.partial(lambda b, s, pt, g: (pt[b, s * n_pg + g], 0), g=g))
               for g in range(n_pg)]
    grid_spec = pltpu.PrefetchScalarGridSpec(
        num_scalar_prefetch=1,
        grid=(nb, n_steps),
        in_specs=[pl.BlockSpec((1, nr, dh), lambda b, s, pt: (b, 0, 0)),
                  pl.BlockSpec((1, nr, dh), lambda b, s, pt: (b, 0, 0)),
                  pl.BlockSpec((1, nh, dv), lambda b, s, pt: (b, 0, 0)),
                  *k_specs, *v_specs,
                  lam_spec, lam_spec, lam_spec, lam_spec,
                  pl.BlockSpec((1, dv), lambda b, s, pt: (0, 0))],
        out_specs=pl.BlockSpec((1, nh, dv), lambda b, s, pt: (b, 0, 0)),
        scratch_shapes=[pltpu.VMEM((nr, dv), F32),
                        pltpu.VMEM((2, nr, LANE), F32)],
    )
    out = pl.pallas_call(
        kern,
        grid_spec=grid_spec,
        out_shape=jax.ShapeDtypeStruct((nb, nh, dv), BF16),
        compiler_params=_cparams(("arbitrary", "arbitrary")),
        name="attn_decode",
    )(page_ids, rows_ch(q), rows_ch(k_new), v_new.reshape(nb, nh, dv),
      *([k2] * n_pg), *([v2] * n_pg), *[l.reshape(1, dh) for l in lams], subln_w.reshape(1, dv))
    return out.reshape(nb, nh * dv)


def _pick(m, pref):
    return pref if m % pref == 0 else m


def _group(x, mods, rows_per_mod, weights, mixer, q_dtype, q_scale):
    (norm1_w, w_in_b, n_m, w_g, w_aq, w_ak, w_av, q_norm_w, k_norm_w, w_out_m, w_out_a,
     norm2_w, w_up, w_down) = weights
    sh1, sc1, g1, sh2, sc2, g2 = mods
    m, d = x.shape
    unit = m if rows_per_mod == 1 else rows_per_mod
    tm = _pick(unit, 1024)
    tr = _pick(unit, 256)
    h = _norm_mod(x, norm1_w, sc1, sh1, rows_per_mod, tr)
    proj_m = _mm([(h, w_in_b)], n_out=n_m, tm=tm, tn=512, name="in_mlstm")
    gcol, grow = _gate_proj(h, w_g, _pick(m, 512))
    qa = _mm([(h, w_aq)], epilogue="qknorm", normw=q_norm_w, post_scale=q_scale, out_dtype=q_dtype,
             tm=tm, tn=512, name="in_aq")
    ka = _mm([(h, w_ak)], epilogue="qknorm", normw=k_norm_w, tm=tm, tn=512, name="in_ak")
    va = _mm([(h, w_av)], tm=tm, tn=512, name="in_av")
    hm, oa, state = mixer(proj_m, gcol, grow, qa, ka, va)
    x1 = _mm([(hm, w_out_m), (oa, w_out_a)], epilogue="resid", resid=x, gate=g1,
             rows_per_gate=rows_per_mod, tm=tm, tn=512, name="out_proj")
    h2 = _norm_mod(x1, norm2_w, sc2, sh2, rows_per_mod, tr)
    u = _mm([(h2, w_up)], epilogue="relu2", out_dtype=BF16, tm=tm, tn=512, name="ffn_up")
    y = _mm([(u, w_down)], epilogue="resid", resid=x1, gate=g2, rows_per_gate=rows_per_mod,
            tm=tm, tn=_pick(d, 1024), tk=2048, name="ffn_down")
    return y, ka, va, state


def kernel(x_prompt, x_sample, cache_k, cache_v, state_C, state_n, state_m, page_table,
           c_prompt, c_sample, w_ada, b_ada, norm1_w, w_in, b_igate, b_fgate, mh_norm_w,
           q_norm_w, k_norm_w, lambda_q1, lambda_k1, lambda_q2, lambda_k2, subln_w,
           w_out, norm2_w, w_up, w_down):
    depth = w_ada.shape[0]
    batch, seq, d = x_prompt.shape
    nb = x_sample.shape[0]
    dk = state_C.shape[3]
    dv = state_C.shape[4]
    dh = cache_k.shape[-1]
    adv = cache_v.shape[-1]
    mq, mv = M_HEADS * dk, M_HEADS * dv
    aq, av = A_HEADS * 2 * dh, A_HEADS * adv
    n_pool = cache_k.shape[1]
    n_m = 2 * mq + 2 * mv
    o_attn = n_m + 2 * M_HEADS

    yp = x_prompt.reshape(batch * seq, d)
    ys = x_sample.reshape(nb, d)
    outs = [[] for _ in range(10)]
    for l in range(depth):
        lam_init = 0.8 - 0.6 * math.exp(-0.3 * l)
        lams = (lambda_q1[l], lambda_k1[l], lambda_q2[l], lambda_k2[l])

        c_all = jnp.concatenate([c_prompt, c_sample], axis=0)
        pad = (-c_all.shape[0]) % 8
        mod = _ada(jnp.pad(c_all, ((0, pad), (0, 0))), w_ada[l], b_ada[l])
        mods_p = tuple(mod[:batch, i * d:(i + 1) * d] for i in range(N_MOD))
        mods_s = tuple(mod[batch:batch + nb, i * d:(i + 1) * d] for i in range(N_MOD))

        w_in_b = w_in[l].astype(BF16)
        w_g = w_in[l][:, n_m:o_attn]
        w_aq = w_in_b[:, o_attn:o_attn + aq]
        w_ak = w_in_b[:, o_attn + aq:o_attn + 2 * aq]
        w_av = w_in_b[:, o_attn + 2 * aq:o_attn + 2 * aq + av]
        w_out_b = w_out[l].astype(BF16)
        weights = (norm1_w[l], w_in_b, n_m, w_g, w_aq, w_ak, w_av, q_norm_w[l], k_norm_w[l],
                   w_out_b[:mv], w_out_b[mv:], norm2_w[l], w_up[l].astype(BF16), w_down[l].astype(BF16))

        def prompt_mixer(proj_m, gcol, grow, qa, ka, va):
            hm, c_new, n_new, m_new = _mlstm_prompt(proj_m, gcol, grow, b_igate[l], b_fgate[l],
                                                    mh_norm_w[l], batch, seq, dk, dv)
            oa = _attn_prompt(qa, ka, va, lams, subln_w[l], batch, seq, dh, adv, lam_init)
            return hm, oa, (c_new, n_new[..., 0], m_new[:, 0, :M_HEADS])

        def sample_mixer(proj_m, gcol, grow, qa, ka, va):
            hm, c_new, n_new, m_new = _mlstm_decode(proj_m, gcol, b_igate[l], b_fgate[l], mh_norm_w[l],
                                                    state_C, state_n, state_m[l], l, dk, dv)
            oa = _attn_decode(page_table + l * n_pool, qa, ka, va, cache_k, cache_v,
                              lams, subln_w[l], dh, adv, lam_init)
            return hm.reshape(nb, mv), oa, (c_new, n_new, m_new[:, 0, :M_HEADS])

        yp, kp, vp, (cp, np_, mp) = _group(yp, mods_p, seq, weights, prompt_mixer, BF16, dh ** -0.5 * LOG2E)
        ys, ks, vs, (cs, ns, ms) = _group(ys, mods_s, 1, weights, sample_mixer, F32, 1.0)
        for lst, val in zip(outs, (kp.reshape(batch, seq, A_HEADS, 2, dh), vp.reshape(batch, seq, A_HEADS, adv),
                                   cp, np_, mp,
                                   ks.reshape(nb, 1, A_HEADS, 2, dh), vs.reshape(nb, 1, A_HEADS, adv),
                                   cs, ns, ms)):
            lst.append(val)
    stacked = [jnp.stack(o) for o in outs]
    return (yp.reshape(batch, seq, d), ys.reshape(nb, 1, d), *stacked)
```

```python
import functools
import math

import jax
import jax.numpy as jnp
from jax import lax
from jax.experimental import pallas as pl
from jax.experimental.pallas import tpu as pltpu

F32 = jnp.float32
BF16 = jnp.bfloat16
EPS = 1e-6
LOG2E = math.log2(math.e)
LANE = 128
VMEM_LIMIT = 56 * 1024 * 1024

M_HEADS = 4
A_HEADS = 8
N_MOD = 6
M_CHUNK = 256
ATTN_TQ = 512
DECODE_PAGES = 8


def _cparams(sem):
    return pltpu.CompilerParams(dimension_semantics=sem, vmem_limit_bytes=VMEM_LIMIT)


def _dot(a, b):
    return jnp.dot(a, b, preferred_element_type=F32)


def _dot_nt(a, b):
    return lax.dot_general(a, b, (((1,), (1,)), ((), ())), preferred_element_type=F32)


def _dot_tn(a, b):
    return lax.dot_general(a, b, (((0,), (0,)), ((), ())), preferred_element_type=F32)


def _log_sigmoid(x):
    return jnp.minimum(x, 0.0) - jnp.log(1.0 + jnp.exp(-jnp.abs(x)))


def _sigmoid(x):
    return 1.0 / (1.0 + jnp.exp(-x))


def _ada_kernel(c_ref, w_ref, b_ref, o_ref):
    c = c_ref[...]
    a = (c * _sigmoid(c)).astype(BF16)
    o_ref[...] = _dot(a, w_ref[...].astype(BF16)) + b_ref[...]


def _ada(c, w, b, tn=512):
    m, k = c.shape
    n = w.shape[1]
    return pl.pallas_call(
        _ada_kernel,
        grid=(n // tn,),
        in_specs=[pl.BlockSpec((m, k), lambda j: (0, 0)),
                  pl.BlockSpec((k, tn), lambda j: (0, j)),
                  pl.BlockSpec((1, tn), lambda j: (0, j))],
        out_specs=pl.BlockSpec((m, tn), lambda j: (0, j)),
        out_shape=jax.ShapeDtypeStruct((m, n), F32),
        compiler_params=_cparams(("arbitrary",)),
        name="ada",
    )(c, w, b.reshape(1, n))


def _norm_mod_kernel(x_ref, w_ref, sc_ref, sh_ref, o_ref):
    x = x_ref[...]
    sc = sc_ref[...].reshape(-1, x.shape[-1])
    sh = sh_ref[...].reshape(-1, x.shape[-1])
    y = x * lax.rsqrt(jnp.mean(x * x, axis=-1, keepdims=True) + EPS) * w_ref[...]
    o_ref[...] = (y * (1.0 + sc) + sh).astype(o_ref.dtype)


def _norm_mod(x, w, sc, sh, rows_per_mod, tr):
    m, d = x.shape
    if rows_per_mod == 1:
        mod_spec = pl.BlockSpec((tr, d), lambda i: (i, 0))
    else:
        per = rows_per_mod // tr
        sc = sc.reshape(-1, 1, d)
        sh = sh.reshape(-1, 1, d)
        mod_spec = pl.BlockSpec((1, 1, d), lambda i: (i // per, 0, 0))
    return pl.pallas_call(
        _norm_mod_kernel,
        grid=(m // tr,),
        in_specs=[pl.BlockSpec((tr, d), lambda i: (i, 0)),
                  pl.BlockSpec((1, d), lambda i: (0, 0)),
                  mod_spec, mod_spec],
        out_specs=pl.BlockSpec((tr, d), lambda i: (i, 0)),
        out_shape=jax.ShapeDtypeStruct((m, d), BF16),
        compiler_params=_cparams(("arbitrary",)),
        name="norm_mod",
    )(x, w.reshape(1, d), sc, sh)


def _mm_kernel(*refs, n_pairs, epilogue, nk, post_scale, rows_out):
    ab = refs[:2 * n_pairs]
    rest = refs[2 * n_pairs:]
    if nk > 1:
        acc_ref = rest[-1]
        rest = rest[:-1]
    if rows_out:
        o3_ref = rest[-1]
        rest = rest[:-1]
    o_ref = rest[-1]
    extras = rest[:-1]

    def partial_product():
        r = _dot(ab[0][...], ab[1][...])
        for p in range(1, n_pairs):
            r = r + _dot(ab[2 * p][...], ab[2 * p + 1][...])
        return r

    def finish(acc):
        if epilogue == "plain":
            o_ref[...] = acc.astype(o_ref.dtype)
        elif epilogue == "relu2":
            r = jnp.maximum(acc, 0.0)
            o_ref[...] = (r * r).astype(o_ref.dtype)
        elif epilogue == "qknorm":
            w = extras[0][...]
            for j in range(acc.shape[-1] // LANE):
                blk = acc[:, j * LANE:(j + 1) * LANE]
                y = blk * lax.rsqrt(jnp.mean(blk * blk, axis=-1, keepdims=True) + EPS) * w
                if post_scale != 1.0:
                    y = y * post_scale
                o_ref[:, j * LANE:(j + 1) * LANE] = y.astype(o_ref.dtype)
                if rows_out:
                    o3_ref[:, j, :] = y.astype(o3_ref.dtype)
        elif epilogue == "resid":
            x = extras[0][...]
            g = extras[1][...].reshape(-1, acc.shape[-1])
            o_ref[...] = (x + g * acc).astype(o_ref.dtype)
        else:
            raise ValueError(epilogue)

    if nk == 1:
        finish(partial_product())
    else:
        k = pl.program_id(2)

        @pl.when(k == 0)
        def _():
            acc_ref[...] = partial_product()

        @pl.when(k > 0)
        def _():
            acc_ref[...] += partial_product()

        @pl.when(k == nk - 1)
        def _():
            finish(acc_ref[...])


def _mm(pairs, *, epilogue="plain", out_dtype=F32, tm, tn, tk=None, n_out=None, w_row0=None,
        normw=None, post_scale=1.0, rows_out=False, resid=None, gate=None, rows_per_gate=1, name="mm"):
    m, kdim = pairs[0][0].shape
    n = pairs[0][1].shape[1] if n_out is None else n_out
    tk = kdim if tk is None else tk
    nk = kdim // tk
    w_row0 = [0] * len(pairs) if w_row0 is None else w_row0
    assert m % tm == 0 and n % tn == 0 and kdim % tk == 0
    grid = (m // tm, n // tn, nk)
    in_specs, args = [], []
    for (a, w), r0 in zip(pairs, w_row0):
        assert a.shape == (m, kdim) and w.shape[0] >= r0 + kdim and w.shape[1] >= n and r0 % tk == 0
        in_specs += [pl.BlockSpec((tm, tk), lambda i, j, k: (i, k)),
                     pl.BlockSpec((tk, tn), functools.partial(lambda i, j, k, kb: (k + kb, j), kb=r0 // tk))]
        args += [a, w]
    if epilogue == "qknorm":
        in_specs.append(pl.BlockSpec((1, LANE), lambda i, j, k: (0, 0)))
        args.append(normw.reshape(1, LANE))
    if epilogue == "resid":
        in_specs.append(pl.BlockSpec((tm, tn), lambda i, j, k: (i, j)))
        args.append(resid)
        if rows_per_gate == 1:
            in_specs.append(pl.BlockSpec((tm, tn), lambda i, j, k: (i, j)))
            args.append(gate)
        else:
            per = rows_per_gate // tm
            in_specs.append(pl.BlockSpec((1, 1, tn), lambda i, j, k: (i // per, 0, j)))
            args.append(gate.reshape(-1, 1, n))
    scratch = [pltpu.VMEM((tm, tn), F32)] if nk > 1 else []
    out_specs = pl.BlockSpec((tm, tn), lambda i, j, k: (i, j))
    out_shape = jax.ShapeDtypeStruct((m, n), out_dtype)
    if rows_out:
        assert epilogue == "qknorm" and (tn // LANE) % 8 == 0
        out_specs = [out_specs, pl.BlockSpec((tm, tn // LANE, LANE), lambda i, j, k: (i, j, 0))]
        out_shape = [out_shape, jax.ShapeDtypeStruct((m, n // LANE, LANE), out_dtype)]
    return pl.pallas_call(
        functools.partial(_mm_kernel, n_pairs=len(pairs), epilogue=epilogue, nk=nk, post_scale=post_scale,
                          rows_out=rows_out),
        grid=grid,
        in_specs=in_specs,
        out_specs=out_specs,
        out_shape=out_shape,
        scratch_shapes=scratch,
        compiler_params=_cparams(("arbitrary", "arbitrary", "arbitrary")),
        name=name,
    )(*args)


def _cast_mm_kernel(a_ref, w_ref, *rest, epilogue, nk):
    if nk > 1:
        acc_ref = rest[-1]
        rest = rest[:-1]
    wb_ref, o_ref = rest[-2:]
    extras = rest[:-2]
    wb = w_ref[...].astype(BF16)
    wb_ref[...] = wb
    part = _dot(a_ref[...], wb)

    def finish(acc):
        if epilogue == "relu2":
            r = jnp.maximum(acc, 0.0)
            o_ref[...] = (r * r).astype(o_ref.dtype)
        elif epilogue == "resid":
            o_ref[...] = (extras[0][...] + extras[1][...] * acc).astype(o_ref.dtype)
        else:
            raise ValueError(epilogue)

    if nk == 1:
        finish(part)
    else:
        k = pl.program_id(1)

        @pl.when(k == 0)
        def _():
            acc_ref[...] = part

        @pl.when(k > 0)
        def _():
            acc_ref[...] += part

        @pl.when(k == nk - 1)
        def _():
            finish(acc_ref[...])


def _cast_mm(a, w, *, epilogue, out_dtype, tn, tk=None, resid=None, gate=None, name="cast_mm"):
    m, kdim = a.shape
    n = w.shape[1]
    tk = kdim if tk is None else tk
    nk = kdim // tk
    assert w.shape[0] == kdim and n % tn == 0 and kdim % tk == 0
    in_specs = [pl.BlockSpec((m, tk), lambda j, k: (0, k)),
                pl.BlockSpec((tk, tn), lambda j, k: (k, j))]
    args = [a, w]
    if epilogue == "resid":
        in_specs += [pl.BlockSpec((m, tn), lambda j, k: (0, j))] * 2
        args += [resid, gate]
    return pl.pallas_call(
        functools.partial(_cast_mm_kernel, epilogue=epilogue, nk=nk),
        grid=(n // tn, nk),
        in_specs=in_specs,
        out_specs=[pl.BlockSpec((tk, tn), lambda j, k: (k, j)),
                   pl.BlockSpec((m, tn), lambda j, k: (0, j))],
        out_shape=[jax.ShapeDtypeStruct((kdim, n), BF16),
                   jax.ShapeDtypeStruct((m, n), out_dtype)],
        scratch_shapes=[pltpu.VMEM((m, tn), F32)] if nk > 1 else [],
        compiler_params=_cparams(("arbitrary", "arbitrary")),
        name=name,
    )(*args)


def _gate_kernel(h_ref, wc_ref, wr_ref, oc_ref, or_ref):
    h = h_ref[...]
    oc_ref[...] = _dot(h, wc_ref[...])
    or_ref[...] = _dot_nt(wr_ref[...], h)


def _gate_proj(h, w_gate_cols, tm):
    m, k = h.shape
    wc = jnp.pad(w_gate_cols, ((0, 0), (0, LANE - 8))).astype(BF16)
    wr = jnp.pad(w_gate_cols.T, ((0, 8), (0, 0))).astype(BF16)
    return pl.pallas_call(
        _gate_kernel,
        grid=(m // tm,),
        in_specs=[pl.BlockSpec((tm, k), lambda i: (i, 0)),
                  pl.BlockSpec((k, LANE), lambda i: (0, 0)),
                  pl.BlockSpec((16, k), lambda i: (0, 0))],
        out_specs=[pl.BlockSpec((tm, LANE), lambda i: (i, 0)),
                   pl.BlockSpec((16, tm), lambda i: (0, i))],
        out_shape=[jax.ShapeDtypeStruct((m, LANE), F32),
                   jax.ShapeDtypeStruct((16, m), F32)],
        compiler_params=_cparams(("arbitrary",)),
        name="gate_proj",
    )(h, wc, wr)


def _mlstm_prompt_kernel(proj_ref, gc_ref, gr_ref, bc_ref, br_ref, nw_ref,
                         hm_ref, c_out_ref, n_out_ref, m_out_ref,
                         cext_ref, m_ref, *, dk, dv, nc):
    c_idx = pl.program_id(1)
    L = gc_ref.shape[0]
    nh = M_HEADS
    scale = dk ** -0.5

    @pl.when(c_idx == 0)
    def _():
        cext_ref[...] = jnp.zeros_like(cext_ref)
        m_ref[...] = jnp.zeros_like(m_ref)

    row = lax.broadcasted_iota(jnp.int32, (L, L), 0)
    col = lax.broadcasted_iota(jnp.int32, (L, L), 1)
    causal = row >= col
    tril = causal.astype(F32)

    gcol = gc_ref[...] + bc_ref[...]
    grow = gr_ref[...] + br_ref[...]
    lf_col = _log_sigmoid(gcol)
    lf_row = _log_sigmoid(grow)
    bt_col_all = jnp.dot(tril, lf_col, preferred_element_type=F32, precision=lax.Precision.HIGHEST)
    bt_row_all = lax.dot_general(lf_row, tril, (((1,), (1,)), ((), ())),
                                 preferred_element_type=F32, precision=lax.Precision.HIGHEST)

    lane = lax.broadcasted_iota(jnp.int32, (1, LANE), 1)
    m_new_row = jnp.zeros((1, LANE), F32)
    ones_pad = jnp.ones((L, LANE), BF16)

    for h in range(nh):
        it_c = gcol[:, h:h + 1]
        bt_c = bt_col_all[:, nh + h:nh + h + 1]
        it_r = grow[h:h + 1, :]
        bt_r = bt_row_all[nh + h:nh + h + 1, :]
        m0 = m_ref[h][0:1, 0:1]

        log_d = jnp.where(causal, bt_c - bt_r + it_r, -jnp.inf)
        m_inter = bt_c + m0
        m = jnp.maximum(m_inter, jnp.max(log_d, axis=-1, keepdims=True))
        d_m = jnp.exp(log_d - m)
        g = jnp.exp(m_inter - m)

        q = proj_ref[:, h * dk:(h + 1) * dk].astype(BF16)
        k_f = proj_ref[:, nh * dk + h * dk:nh * dk + (h + 1) * dk] * scale
        v_off = 2 * nh * dk
        v = proj_ref[:, v_off + h * dv:v_off + (h + 1) * dv].astype(BF16)
        vext = jnp.concatenate([v, ones_pad], axis=-1)
        o_off = v_off + nh * dv
        o_gate = proj_ref[:, o_off + h * dv:o_off + (h + 1) * dv]

        s = _dot_nt(q, k_f.astype(BF16)) * d_m
        cext = cext_ref[h]
        numext = _dot(s.astype(BF16), vext) + _dot(q, cext.astype(BF16)) * g
        num = numext[:, :dv]
        den = numext[:, dv:dv + 1]
        den = jnp.maximum(jnp.abs(den), jnp.exp(-m))
        hh = num / den
        y = hh * lax.rsqrt(jnp.mean(hh * hh, axis=-1, keepdims=True) + EPS) * nw_ref[:, h * dv:(h + 1) * dv]
        hm_ref[:, h * dv:(h + 1) * dv] = (_sigmoid(o_gate) * y).astype(hm_ref.dtype)

        m_last = m[L - 1:L, :]
        bt_last = bt_c[L - 1:L, :]
        w_c = jnp.exp(bt_last - bt_c + it_c - m_last)
        g_last = jnp.exp(bt_last + m0 - m_last)
        kw = (k_f * w_c).astype(BF16)
        cext_ref[h] = g_last * cext + _dot_tn(kw, vext)
        m_ref[h] = jnp.broadcast_to(m_last, m_ref.shape[1:])
        m_new_row = jnp.where(lane == h, m_last, m_new_row)

    @pl.when(c_idx == nc - 1)
    def _():
        for h in range(nh):
            c_out_ref[0, h] = cext_ref[h][:, :dv]
            n_out_ref[0, h] = cext_ref[h][:, dv:]
        m_out_ref[0] = m_new_row


def _mlstm_prompt(proj, gcol, grow, b_ig, b_fg, mh_norm_w, batch, seq, dk, dv):
    nh = M_HEADS
    L = min(M_CHUNK, seq)
    nc = seq // L
    m = batch * seq
    width = proj.shape[1]
    bias = jnp.concatenate([b_ig, b_fg]).astype(F32)
    bias_cols = jnp.pad(bias, (0, LANE - 2 * nh)).reshape(1, LANE)
    bias_rows = jnp.pad(bias, (0, 16 - 2 * nh)).reshape(16, 1)
    kern = functools.partial(_mlstm_prompt_kernel, dk=dk, dv=dv, nc=nc)
    return pl.pallas_call(
        kern,
        grid=(batch, nc),
        in_specs=[pl.BlockSpec((L, width), lambda b, c: (b * nc + c, 0)),
                  pl.BlockSpec((L, LANE), lambda b, c: (b * nc + c, 0)),
                  pl.BlockSpec((16, L), lambda b, c: (0, b * nc + c)),
                  pl.BlockSpec((1, LANE), lambda b, c: (0, 0)),
                  pl.BlockSpec((16, 1), lambda b, c: (0, 0)),
                  pl.BlockSpec((1, nh * dv), lambda b, c: (0, 0))],
        out_specs=[pl.BlockSpec((L, nh * dv), lambda b, c: (b * nc + c, 0)),
                   pl.BlockSpec((1, nh, dk, dv), lambda b, c: (b, 0, 0, 0)),
                   pl.BlockSpec((1, nh, dk, LANE), lambda b, c: (b, 0, 0, 0)),
                   pl.BlockSpec((1, 1, LANE), lambda b, c: (b, 0, 0))],
        out_shape=[jax.ShapeDtypeStruct((m, nh * dv), BF16),
                   jax.ShapeDtypeStruct((batch, nh, dk, dv), F32),
                   jax.ShapeDtypeStruct((batch, nh, dk, LANE), F32),
                   jax.ShapeDtypeStruct((batch, 1, LANE), F32)],
        scratch_shapes=[pltpu.VMEM((nh, dk, dv + LANE), F32),
                        pltpu.VMEM((nh, 8, LANE), F32)],
        compiler_params=_cparams(("arbitrary", "arbitrary")),
        name="mlstm_prompt",
    )(proj, gcol, grow, bias_cols, bias_rows, mh_norm_w.reshape(1, nh * dv))


def _mlstm_decode_kernel(proj_ref, g_ref, bias_ref, nw_ref, c0_ref, n0_ref, m0_ref,
                         hm_ref, c_out_ref, n_out_ref, m_out_ref, *, dk, dv):
    nh = M_HEADS
    scale = dk ** -0.5
    gates = g_ref[0] + bias_ref[...]
    lf_all = _log_sigmoid(gates)
    m0_all = m0_ref[0]
    lane = lax.broadcasted_iota(jnp.int32, (1, LANE), 1)
    eye = (lax.broadcasted_iota(jnp.int32, (dk, dk), 0)
           == lax.broadcasted_iota(jnp.int32, (dk, dk), 1))
    m_new_row = jnp.zeros((1, LANE), F32)

    def to_col(r):
        return jnp.sum(jnp.where(eye, jnp.broadcast_to(r, (dk, dk)), 0.0), axis=1, keepdims=True)

    v_off = 2 * nh * dk
    o_off = v_off + nh * dv
    for h in range(nh):
        it = gates[:, h:h + 1]
        lf = lf_all[:, nh + h:nh + h + 1]
        m0 = m0_all[:, h:h + 1]
        q = proj_ref[0, :, h * dk:(h + 1) * dk]
        k = proj_ref[0, :, nh * dk + h * dk:nh * dk + (h + 1) * dk] * scale
        v = proj_ref[0, :, v_off + h * dv:v_off + (h + 1) * dv]
        o_gate = proj_ref[0, :, o_off + h * dv:o_off + (h + 1) * dv]
        c0 = c0_ref[0, h]
        n0 = n0_ref[0, h:h + 1, :]

        m_inter = lf + m0
        m = jnp.maximum(m_inter, it)
        d_m = jnp.exp(it - m)
        g = jnp.exp(m_inter - m)
        s = jnp.sum(q * k, axis=-1, keepdims=True) * d_m
        q_col = to_col(q)
        qc = jnp.sum(q_col * c0, axis=0, keepdims=True)
        num = s * v + qc * g
        den = s + jnp.sum(q * n0, axis=-1, keepdims=True) * g
        den = jnp.maximum(jnp.abs(den), jnp.exp(-m))
        hh = num / den
        y = hh * lax.rsqrt(jnp.mean(hh * hh, axis=-1, keepdims=True) + EPS) * nw_ref[:, h * dv:(h + 1) * dv]
        hm_ref[0, :, h * dv:(h + 1) * dv] = (_sigmoid(o_gate) * y).astype(hm_ref.dtype)

        kw = k * d_m
        c_out_ref[0, h] = g * c0 + to_col(kw) * v
        n_out_ref[0, h:h + 1, :] = g * n0 + kw
        m_new_row = jnp.where(lane == h, m, m_new_row)
    m_out_ref[0] = m_new_row


def _mlstm_decode(proj, gcol, b_ig, b_fg, mh_norm_w, c_all, n_all, m0, layer, dk, dv):
    nh = M_HEADS
    nb, width = proj.shape
    bias = jnp.pad(jnp.concatenate([b_ig, b_fg]).astype(F32), (0, LANE - 2 * nh)).reshape(1, LANE)
    kern = functools.partial(_mlstm_decode_kernel, dk=dk, dv=dv)
    c_flat = c_all.reshape(-1, nh, dk, dv)
    n_flat = n_all.reshape(-1, nh, dk)
    return pl.pallas_call(
        kern,
        grid=(nb,),
        in_specs=[pl.BlockSpec((1, 1, width), lambda b: (b, 0, 0)),
                  pl.BlockSpec((1, 1, LANE), lambda b: (b, 0, 0)),
                  pl.BlockSpec((1, LANE), lambda b: (0, 0)),
                  pl.BlockSpec((1, nh * dv), lambda b: (0, 0)),
                  pl.BlockSpec((1, nh, dk, dv), lambda b: (layer * nb + b, 0, 0, 0)),
                  pl.BlockSpec((1, nh, dk), lambda b: (layer * nb + b, 0, 0)),
                  pl.BlockSpec((1, 1, nh), lambda b: (b, 0, 0))],
        out_specs=[pl.BlockSpec((1, 1, nh * dv), lambda b: (b, 0, 0)),
                   pl.BlockSpec((1, nh, dk, dv), lambda b: (b, 0, 0, 0)),
                   pl.BlockSpec((1, nh, dk), lambda b: (b, 0, 0)),
                   pl.BlockSpec((1, 1, LANE), lambda b: (b, 0, 0))],
        out_shape=[jax.ShapeDtypeStruct((nb, 1, nh * dv), BF16),
                   jax.ShapeDtypeStruct((nb, nh, dk, dv), F32),
                   jax.ShapeDtypeStruct((nb, nh, dk), F32),
                   jax.ShapeDtypeStruct((nb, 1, LANE), F32)],
        compiler_params=_cparams(("arbitrary",)),
        name="mlstm_decode",
    )(proj.reshape(nb, 1, width), gcol.reshape(nb, 1, LANE), bias,
      mh_norm_w.reshape(1, nh * dv), c_flat, n_flat, m0.reshape(nb, 1, nh))


def _lambda_full(lq1, lk1, lq2, lk2, lam_init):
    a = jnp.exp(jnp.sum(lq1 * lk1, axis=-1, keepdims=True))
    b = jnp.exp(jnp.sum(lq2 * lk2, axis=-1, keepdims=True))
    return a - b + lam_init


def _lane_fold(x, op):
    out = x[:, :LANE]
    for i in range(1, x.shape[-1] // LANE):
        out = op(out, x[:, i * LANE:(i + 1) * LANE])
    return out


def _attn_prompt_kernel(q_ref, k_ref, v_ref, lq1_ref, lk1_ref, lq2_ref, lk2_ref, sw_ref,
                        o_ref, kb_ref, vb_ref, s_ref, p_ref, *, dh, lam_init, tq):
    seq = q_ref.shape[0]
    kb_ref[...] = k_ref[...].astype(BF16)
    vb_ref[...] = v_ref[...].astype(BF16)
    lam = _lambda_full(lq1_ref[...], lk1_ref[...], lq2_ref[...], lk2_ref[...], lam_init)
    causal = (lax.broadcasted_iota(jnp.int32, (tq, tq), 0)
              >= lax.broadcasted_iota(jnp.int32, (tq, tq), 1))
    for qi in range(seq // tq):
        kv_len = (qi + 1) * tq
        outs = []
        for c in range(2):
            q_c = q_ref[qi * tq:(qi + 1) * tq, c * dh:(c + 1) * dh]
            mx = None
            for j in range(qi + 1):
                s = _dot_nt(q_c, kb_ref[j * tq:(j + 1) * tq, c * dh:(c + 1) * dh])
                if j == qi:
                    s = jnp.where(causal, s, -jnp.inf)
                s_ref[:, j * tq:(j + 1) * tq] = s
                part = _lane_fold(s, jnp.maximum)
                mx = part if mx is None else jnp.maximum(mx, part)
            m = jnp.max(mx, axis=-1, keepdims=True)
            ls = None
            for j in range(qi + 1):
                p = jnp.exp2(s_ref[:, j * tq:(j + 1) * tq] - m)
                part = _lane_fold(p, jnp.add)
                ls = part if ls is None else ls + part
                p_ref[:, j * tq:(j + 1) * tq] = p.astype(BF16)
            l = jnp.sum(ls, axis=-1, keepdims=True)
            outs.append(_dot(p_ref[:, :kv_len], vb_ref[:kv_len, :]) / l)
        o = outs[0] - lam * outs[1]
        y = o * lax.rsqrt(jnp.mean(o * o, axis=-1, keepdims=True) + EPS) * sw_ref[...]
        o_ref[qi * tq:(qi + 1) * tq, :] = (y * (1.0 - lam_init)).astype(o_ref.dtype)


def _attn_prompt(q, k, v, lams, subln_w, batch, seq, dh, dv, lam_init):
    tq = min(ATTN_TQ, seq)
    m = batch * seq
    kern = functools.partial(_attn_prompt_kernel, dh=dh, lam_init=lam_init, tq=tq)
    lam_spec = pl.BlockSpec((1, dh), lambda b, h: (0, 0))
    return pl.pallas_call(
        kern,
        grid=(batch, A_HEADS),
        in_specs=[pl.BlockSpec((seq, 2 * dh), lambda b, h: (b, h)),
                  pl.BlockSpec((seq, 2 * dh), lambda b, h: (b, h)),
                  pl.BlockSpec((seq, dv), lambda b, h: (b, h)),
                  lam_spec, lam_spec, lam_spec, lam_spec,
                  pl.BlockSpec((1, dv), lambda b, h: (0, 0))],
        out_specs=pl.BlockSpec((seq, dv), lambda b, h: (b, h)),
        out_shape=jax.ShapeDtypeStruct((m, A_HEADS * dv), BF16),
        scratch_shapes=[pltpu.VMEM((seq, 2 * dh), BF16),
                        pltpu.VMEM((seq, dv), BF16),
                        pltpu.VMEM((tq, seq), F32),
                        pltpu.VMEM((tq, seq), BF16)],
        compiler_params=_cparams(("arbitrary", "arbitrary")),
        name="attn_prompt",
    )(q, k, v, *[l.reshape(1, dh) for l in lams], subln_w.reshape(1, dv))


def _attn_decode_kernel(pt_ref, q_ref, kn_ref, vn_ref, *rest, dh, dv, lam_init, n_steps, n_pg, page):
    kp_refs = rest[:n_pg]
    vp_refs = rest[n_pg:2 * n_pg]
    lq1_ref, lk1_ref, lq2_ref, lk2_ref, sw_ref, o_ref, acc_ref, ml_ref = rest[2 * n_pg:]
    step = pl.program_id(1)
    nh = A_HEADS
    nr = 2 * nh
    grp = 2 * LANE
    tok_g = grp // nr
    n_grp = page * nr // grp
    qs = q_ref[0] * (dh ** -0.5 * LOG2E)

    @pl.when(step == 0)
    def _():
        acc_ref[...] = jnp.zeros_like(acc_ref)
        ml_ref[0] = jnp.full(ml_ref.shape[1:], -jnp.inf, F32)
        ml_ref[1] = jnp.zeros(ml_ref.shape[1:], F32)

    def rescale(m_blk):
        m_old = ml_ref[0][:, 0:1]
        m_new = jnp.maximum(m_old, m_blk)
        return m_new, jnp.exp2(m_old - m_new)

    def commit(m_new, alpha, l_blk, pv):
        acc_ref[...] = alpha * acc_ref[...] + pv
        ml_ref[1] = jnp.broadcast_to(alpha * ml_ref[1][:, 0:1] + l_blk, ml_ref.shape[1:])
        ml_ref[0] = jnp.broadcast_to(m_new, ml_ref.shape[1:])

    shape3 = (n_pg * n_grp, nr, grp)
    r3 = lax.broadcasted_iota(jnp.int32, shape3, 1)
    c3 = lax.broadcasted_iota(jnp.int32, shape3, 2)
    valid = (c3 % nr) == (r3 % nh) * 2 + r3 // nh
    e_row = lax.broadcasted_iota(jnp.int32, (grp, LANE), 0)
    e_col = lax.broadcasted_iota(jnp.int32, (grp, LANE), 1)
    fold = ((e_row // nr == e_col // nh) & ((e_row % nr) // 2 == e_col % nh)).astype(BF16)

    qb = qs.astype(BF16)
    pieces = []
    for pg in range(n_pg):
        s_pg = _dot_nt(qb, kp_refs[pg][...].astype(BF16))
        pieces += [s_pg[:, g * grp:(g + 1) * grp] for g in range(n_grp)]
    s3 = jnp.where(valid, jnp.concatenate(pieces, axis=0).reshape(shape3), -jnp.inf)
    m_new, alpha = rescale(jnp.max(jnp.max(s3, axis=0), axis=-1, keepdims=True))
    p3 = jnp.exp2(s3 - m_new[None])
    l_blk = jnp.sum(jnp.sum(p3, axis=0), axis=-1, keepdims=True)
    p2 = _dot(p3.reshape(n_pg * n_grp * nr, grp).astype(BF16), fold).astype(BF16)
    pv = None
    for pg in range(n_pg):
        lhs = jnp.concatenate([p2[(pg * n_grp + g) * nr:(pg * n_grp + g + 1) * nr] for g in range(n_grp)], axis=1)
        part = _dot(lhs, vp_refs[pg][...].astype(BF16))
        pv = part if pv is None else pv + part
    commit(m_new, alpha, l_blk, pv)

    @pl.when(step == n_steps - 1)
    def _():
        s_new = jnp.sum(qs * kn_ref[0], axis=-1, keepdims=True)
        m_fin, a_fin = rescale(s_new)
        p_new = jnp.exp2(s_new - m_fin)
        vn = vn_ref[0]
        commit(m_fin, a_fin, p_new, p_new * jnp.concatenate([vn, vn], axis=0))
        lam = _lambda_full(lq1_ref[...], lk1_ref[...], lq2_ref[...], lk2_ref[...], lam_init)
        o_n = acc_ref[...] / ml_ref[1][:, 0:1]
        o = o_n[:nh] - lam * o_n[nh:]
        y = o * lax.rsqrt(jnp.mean(o * o, axis=-1, keepdims=True) + EPS) * sw_ref[...]
        o_ref[0] = (y * (1.0 - lam_init)).astype(o_ref.dtype)


def _attn_decode(page_ids, q, k_new, v_new, cache_k, cache_v, lams, subln_w, dh, dv, lam_init):
    nb, n_pages = page_ids.shape
    page = cache_k.shape[-4]
    nh = A_HEADS
    nr = 2 * nh
    n_pg = math.gcd(DECODE_PAGES, n_pages)
    n_steps = n_pages // n_pg
    k2 = cache_k.reshape(-1, dh)
    v2 = cache_v.reshape(-1, dv)

    def rows_ch(x):
        return x.reshape(nb, nh, 2, dh).transpose(0, 2, 1, 3).reshape(nb, nr, dh)

    kern = functools.partial(_attn_decode_kernel, dh=dh, dv=dv, lam_init=lam_init,
                             n_steps=n_steps, n_pg=n_pg, page=page)
    lam_spec = pl.BlockSpec((1, dh), lambda b, s, pt: (0, 0))
    k_specs = [pl.BlockSpec((page * nr, dh), functools.partial(lambda b, s, pt, g: (pt[b, s * n_pg + g], 0), g=g))
               for g in range(n_pg)]
    v_specs = [pl.BlockSpec((page * nh, dv), functools.partial(lambda b, s, pt, g: (pt[b, s * n_pg + g], 0), g=g))
               for g in range(n_pg)]
    grid_spec = pltpu.PrefetchScalarGridSpec(
        num_scalar_prefetch=1,
        grid=(nb, n_steps),
        in_specs=[pl.BlockSpec((1, nr, dh), lambda b, s, pt: (b, 0, 0)),
                  pl.BlockSpec((1, nr, dh), lambda b, s, pt: (b, 0, 0)),
                  pl.BlockSpec((1, nh, dv), lambda b, s, pt: (b, 0, 0)),
                  *k_specs, *v_specs,
                  lam_spec, lam_spec, lam_spec, lam_spec,
                  pl.BlockSpec((1, dv), lambda b, s, pt: (0, 0))],
        out_specs=pl.BlockSpec((1, nh, dv), lambda b, s, pt: (b, 0, 0)),
        scratch_shapes=[pltpu.VMEM((nr, dv), F32),
                        pltpu.VMEM((2, nr, LANE), F32)],
    )
    out = pl.pallas_call(
        kern,
        grid_spec=grid_spec,
        out_shape=jax.ShapeDtypeStruct((nb, nh, dv), BF16),
        compiler_params=_cparams(("arbitrary", "arbitrary")),
        name="attn_decode",
    )(page_ids, rows_ch(q), rows_ch(k_new), v_new.reshape(nb, nh, dv),
      *([k2] * n_pg), *([v2] * n_pg), *[l.reshape(1, dh) for l in lams], subln_w.reshape(1, dv))
    return out.reshape(nb, nh * dv)


def _pick(m, pref):
    return pref if m % pref == 0 else m


def _row_tiles(m, rows_per_mod):
    unit = m if rows_per_mod == 1 else rows_per_mod
    return _pick(unit, 1024), _pick(unit, 256)


def _in_proj(x, sc1, sh1, rows_per_mod, w, q_dtype, q_scale, tn, k_rows_out):
    norm1_w, w_in_b, n_m, w_g, w_aq, w_ak, w_av, q_norm_w, k_norm_w = w
    m = x.shape[0]
    tm, tr = _row_tiles(m, rows_per_mod)
    h = _norm_mod(x, norm1_w, sc1, sh1, rows_per_mod, tr)
    proj_m = _mm([(h, w_in_b)], n_out=n_m, tm=tm, tn=tn, name="in_mlstm")
    gcol, grow = _gate_proj(h, w_g, _pick(m, 512))
    qa = _mm([(h, w_aq)], epilogue="qknorm", normw=q_norm_w, post_scale=q_scale, out_dtype=q_dtype,
             tm=tm, tn=tn, name="in_aq")
    tm_k = _pick(tm, 512) if k_rows_out else tm
    ka = _mm([(h, w_ak)], epilogue="qknorm", normw=k_norm_w, rows_out=k_rows_out, tm=tm_k, tn=tn, name="in_ak")
    va = _mm([(h, w_av)], tm=tm, tn=tn, name="in_av")
    return proj_m, gcol, grow, qa, ka, va


def kernel(x_prompt, x_sample, cache_k, cache_v, state_C, state_n, state_m, page_table,
           c_prompt, c_sample, w_ada, b_ada, norm1_w, w_in, b_igate, b_fgate, mh_norm_w,
           q_norm_w, k_norm_w, lambda_q1, lambda_k1, lambda_q2, lambda_k2, subln_w,
           w_out, norm2_w, w_up, w_down):
    depth = w_ada.shape[0]
    batch, seq, d = x_prompt.shape
    nb = x_sample.shape[0]
    dk = state_C.shape[3]
    dv = state_C.shape[4]
    dh = cache_k.shape[-1]
    adv = cache_v.shape[-1]
    mq, mv = M_HEADS * dk, M_HEADS * dv
    aq, av = A_HEADS * 2 * dh, A_HEADS * adv
    n_pool = cache_k.shape[1]
    n_m = 2 * mq + 2 * mv
    o_attn = n_m + 2 * M_HEADS

    yp = x_prompt.reshape(batch * seq, d)
    ys = x_sample.reshape(nb, d)
    outs = [[] for _ in range(10)]
    for l in range(depth):
        lam_init = 0.8 - 0.6 * math.exp(-0.3 * l)
        lams = (lambda_q1[l], lambda_k1[l], lambda_q2[l], lambda_k2[l])

        c_all = jnp.concatenate([c_prompt, c_sample], axis=0)
        pad = (-c_all.shape[0]) % 8
        mod = _ada(jnp.pad(c_all, ((0, pad), (0, 0))), w_ada[l], b_ada[l])
        mods_p = tuple(mod[:batch, i * d:(i + 1) * d] for i in range(N_MOD))
        mods_s = tuple(mod[batch:batch + nb, i * d:(i + 1) * d] for i in range(N_MOD))

        w_in_b = w_in[l].astype(BF16)
        w_g = w_in[l][:, n_m:o_attn]
        w_aq = w_in_b[:, o_attn:o_attn + aq]
        w_ak = w_in_b[:, o_attn + aq:o_attn + 2 * aq]
        w_av = w_in_b[:, o_attn + 2 * aq:o_attn + 2 * aq + av]
        w_proj = (norm1_w[l], w_in_b, n_m, w_g, w_aq, w_ak, w_av, q_norm_w[l], k_norm_w[l])

        sh1, sc1, g1, sh2, sc2, g2 = mods_s
        proj_m, gcol, _, qa, ks, vs = _in_proj(ys, sc1, sh1, 1, w_proj, F32, 1.0, 512, False)
        hm, cs, ns, ms = _mlstm_decode(proj_m, gcol, b_igate[l], b_fgate[l], mh_norm_w[l],
                                       state_C, state_n, state_m[l], l, dk, dv)
        oa = _attn_decode(page_table + l * n_pool, qa, ks, vs, cache_k, cache_v,
                          lams, subln_w[l], dh, adv, lam_init)
        mix = jnp.concatenate([hm.reshape(nb, mv), oa], axis=1)
        tr_s = _row_tiles(nb, 1)[1]
        w_out_b, x1 = _cast_mm(mix, w_out[l], epilogue="resid", out_dtype=F32, tn=512,
                               resid=ys, gate=g1, name="out_proj_s")
        h2 = _norm_mod(x1, norm2_w[l], sc2, sh2, 1, tr_s)
        w_up_b, u = _cast_mm(h2, w_up[l], epilogue="relu2", out_dtype=BF16, tn=512, name="ffn_up_s")
        w_down_b, ys = _cast_mm(u, w_down[l], epilogue="resid", out_dtype=F32, tn=_pick(d, 1024), tk=2048,
                                resid=x1, gate=g2, name="ffn_down_s")
        ms = ms[:, 0, :M_HEADS]

        sh1, sc1, g1, sh2, sc2, g2 = mods_p
        tm, tr = _row_tiles(batch * seq, seq)
        proj_m, gcol, grow, qa, (kp, kp_rows), vp = _in_proj(yp, sc1, sh1, seq, w_proj, BF16,
                                                             dh ** -0.5 * LOG2E, 1024, True)
        hm, cp, np_, mp = _mlstm_prompt(proj_m, gcol, grow, b_igate[l], b_fgate[l],
                                        mh_norm_w[l], batch, seq, dk, dv)
        oa = _attn_prompt(qa, kp, vp, lams, subln_w[l], batch, seq, dh, adv, lam_init)
        x1 = _mm([(hm, w_out_b), (oa, w_out_b)], w_row0=[0, mv], epilogue="resid", resid=yp, gate=g1,
                 rows_per_gate=seq, tm=tm, tn=512, name="out_proj")
        h2 = _norm_mod(x1, norm2_w[l], sc2, sh2, seq, tr)
        u = _mm([(h2, w_up_b)], epilogue="relu2", out_dtype=BF16, tm=tm, tn=1024, name="ffn_up")
        yp = _mm([(u, w_down_b)], epilogue="resid", resid=x1, gate=g2, rows_per_gate=seq,
                 tm=tm, tn=_pick(d, 1024), tk=2048, name="ffn_down")

        for lst, val in zip(outs, (kp_rows.reshape(batch, seq, A_HEADS, 2, dh),
                                   vp.reshape(batch, seq, A_HEADS, adv),
                                   cp, np_[..., 0], mp[:, 0, :M_HEADS],
                                   ks.reshape(nb, 1, A_HEADS, 2, dh), vs.reshape(nb, 1, A_HEADS, adv),
                                   cs, ns, ms)):
            lst.append(val)
    stacked = [jnp.stack(o) for o in outs]
    return (yp.reshape(batch, seq, d), ys.reshape(nb, 1, d), *stacked)
```

```python
import functools
import math

import jax
import jax.numpy as jnp
from jax import lax
from jax.experimental import pallas as pl
from jax.experimental.pallas import tpu as pltpu

F32 = jnp.float32
BF16 = jnp.bfloat16
EPS = 1e-6
LOG2E = math.log2(math.e)
LANE = 128
VMEM_LIMIT = 56 * 1024 * 1024
VMEM_LIMIT_BIG = 60 * 1024 * 1024

M_HEADS = 4
A_HEADS = 8
N_MOD = 6
M_CHUNK = 256
ATTN_TQ = 512
DECODE_PAGES = 8


def _cparams(sem, vmem_limit=VMEM_LIMIT):
    return pltpu.CompilerParams(dimension_semantics=sem, vmem_limit_bytes=vmem_limit)


def _dot(a, b):
    return jnp.dot(a, b, preferred_element_type=F32)


def _dot_nt(a, b):
    return lax.dot_general(a, b, (((1,), (1,)), ((), ())), preferred_element_type=F32)


def _dot_tn(a, b):
    return lax.dot_general(a, b, (((0,), (0,)), ((), ())), preferred_element_type=F32)


def _log_sigmoid(x):
    return jnp.minimum(x, 0.0) - jnp.log(1.0 + jnp.exp(-jnp.abs(x)))


def _sigmoid(x):
    return 1.0 / (1.0 + jnp.exp(-x))


def _ada_kernel(c_ref, w_ref, b_ref, o_ref):
    c = c_ref[...]
    a = (c * _sigmoid(c)).astype(BF16)
    o_ref[...] = _dot(a, w_ref[...].astype(BF16)) + b_ref[...]


def _ada(c, w, b, tn=512):
    m, k = c.shape
    n = w.shape[1]
    return pl.pallas_call(
        _ada_kernel,
        grid=(n // tn,),
        in_specs=[pl.BlockSpec((m, k), lambda j: (0, 0)),
                  pl.BlockSpec((k, tn), lambda j: (0, j)),
                  pl.BlockSpec((1, tn), lambda j: (0, j))],
        out_specs=pl.BlockSpec((m, tn), lambda j: (0, j)),
        out_shape=jax.ShapeDtypeStruct((m, n), F32),
        compiler_params=_cparams(("arbitrary",)),
        name="ada",
    )(c, w, b.reshape(1, n))


def _norm_mod_kernel(x_ref, w_ref, sc_ref, sh_ref, o_ref):
    x = x_ref[...]
    sc = sc_ref[...].reshape(-1, x.shape[-1])
    sh = sh_ref[...].reshape(-1, x.shape[-1])
    y = x * lax.rsqrt(jnp.mean(x * x, axis=-1, keepdims=True) + EPS) * w_ref[...]
    o_ref[...] = (y * (1.0 + sc) + sh).astype(o_ref.dtype)


def _norm_mod(x, w, sc, sh, rows_per_mod, tr):
    m, d = x.shape
    if rows_per_mod == 1:
        mod_spec = pl.BlockSpec((tr, d), lambda i: (i, 0))
    else:
        per = rows_per_mod // tr
        sc = sc.reshape(-1, 1, d)
        sh = sh.reshape(-1, 1, d)
        mod_spec = pl.BlockSpec((1, 1, d), lambda i: (i // per, 0, 0))
    return pl.pallas_call(
        _norm_mod_kernel,
        grid=(m // tr,),
        in_specs=[pl.BlockSpec((tr, d), lambda i: (i, 0)),
                  pl.BlockSpec((1, d), lambda i: (0, 0)),
                  mod_spec, mod_spec],
        out_specs=pl.BlockSpec((tr, d), lambda i: (i, 0)),
        out_shape=jax.ShapeDtypeStruct((m, d), BF16),
        compiler_params=_cparams(("arbitrary",)),
        name="norm_mod",
    )(x, w.reshape(1, d), sc, sh)


def _mm_kernel(*refs, n_pairs, epilogue, nk, post_scale, rows_out, w_nt):
    ab = refs[:2 * n_pairs]
    rest = refs[2 * n_pairs:]
    if rows_out:
        o3_ref = rest[-1]
        rest = rest[:-1]
    o_ref = rest[-1]
    extras = rest[:-1]

    def partial_product():
        dot = _dot_nt if w_nt else _dot
        r = dot(ab[0][...], ab[1][...])
        for p in range(1, n_pairs):
            r = r + dot(ab[2 * p][...], ab[2 * p + 1][...])
        return r

    def finish(acc):
        if epilogue == "plain":
            o_ref[...] = acc.astype(o_ref.dtype)
        elif epilogue == "relu2":
            r = jnp.maximum(acc, 0.0)
            o_ref[...] = (r * r).astype(o_ref.dtype)
        elif epilogue == "qknorm":
            w = extras[0][...]
            for j in range(acc.shape[-1] // LANE):
                blk = acc[:, j * LANE:(j + 1) * LANE]
                y = blk * lax.rsqrt(jnp.mean(blk * blk, axis=-1, keepdims=True) + EPS) * w
                if post_scale != 1.0:
                    y = y * post_scale
                o_ref[:, j * LANE:(j + 1) * LANE] = y.astype(o_ref.dtype)
                if rows_out:
                    o3_ref[:, j, :] = y.astype(o3_ref.dtype)
        elif epilogue == "resid":
            x = extras[0][...]
            g = extras[1][...].reshape(-1, acc.shape[-1])
            o_ref[...] = (x + g * acc).astype(o_ref.dtype)
        else:
            raise ValueError(epilogue)

    if nk == 1:
        finish(partial_product())
    else:
        k = pl.program_id(2)
        part = partial_product()

        @pl.when(k == 0)
        def _():
            o_ref[...] = part

        @pl.when((k > 0) & (k < nk - 1))
        def _():
            o_ref[...] += part

        @pl.when(k == nk - 1)
        def _():
            finish(o_ref[...] + part)


def _mm(pairs, *, epilogue="plain", out_dtype=F32, tm, tn, tk=None, n_out=None, w_row0=None, wt_row0=None,
        normw=None, post_scale=1.0, rows_out=False, resid=None, gate=None, rows_per_gate=1,
        vmem_limit=VMEM_LIMIT, name="mm"):
    m, kdim = pairs[0][0].shape
    w_nt = wt_row0 is not None
    n = n_out if (n_out is not None) else pairs[0][1].shape[1]
    tk = kdim if tk is None else tk
    nk = kdim // tk
    w_row0 = [0] * len(pairs) if w_row0 is None else w_row0
    assert m % tm == 0 and n % tn == 0 and kdim % tk == 0
    grid = (m // tm, n // tn, nk)
    in_specs, args = [], []
    for (a, w), r0 in zip(pairs, w_row0):
        in_specs.append(pl.BlockSpec((tm, tk), lambda i, j, k: (i, k)))
        if w_nt:
            assert len(pairs) == 1 and a.shape == (m, kdim) and w.shape[1] == kdim
            assert wt_row0 % tn == 0 and w.shape[0] >= wt_row0 + n
            in_specs.append(pl.BlockSpec((tn, tk), lambda i, j, k: (wt_row0 // tn + j, k)))
        else:
            assert a.shape == (m, kdim) and w.shape[0] >= r0 + kdim and w.shape[1] >= n and r0 % tk == 0
            in_specs.append(pl.BlockSpec((tk, tn), functools.partial(lambda i, j, k, kb: (k + kb, j), kb=r0 // tk)))
        args += [a, w]
    if epilogue == "qknorm":
        in_specs.append(pl.BlockSpec((1, LANE), lambda i, j, k: (0, 0)))
        args.append(normw.reshape(1, LANE))
    if epilogue == "resid":
        in_specs.append(pl.BlockSpec((tm, tn), lambda i, j, k: (i, j)))
        args.append(resid)
        if rows_per_gate == 1:
            in_specs.append(pl.BlockSpec((tm, tn), lambda i, j, k: (i, j)))
            args.append(gate)
        else:
            per = rows_per_gate // tm
            in_specs.append(pl.BlockSpec((1, 1, tn), lambda i, j, k: (i // per, 0, j)))
            args.append(gate.reshape(-1, 1, n))
    assert nk == 1 or out_dtype == F32
    out_specs = pl.BlockSpec((tm, tn), lambda i, j, k: (i, j))
    out_shape = jax.ShapeDtypeStruct((m, n), out_dtype)
    if rows_out:
        assert epilogue == "qknorm" and (tn // LANE) % 8 == 0
        out_specs = [out_specs, pl.BlockSpec((tm, tn // LANE, LANE), lambda i, j, k: (i, j, 0))]
        out_shape = [out_shape, jax.ShapeDtypeStruct((m, n // LANE, LANE), out_dtype)]
    return pl.pallas_call(
        functools.partial(_mm_kernel, n_pairs=len(pairs), epilogue=epilogue, nk=nk, post_scale=post_scale,
                          rows_out=rows_out, w_nt=w_nt),
        grid=grid,
        in_specs=in_specs,
        out_specs=out_specs,
        out_shape=out_shape,
        compiler_params=_cparams(("arbitrary", "arbitrary", "arbitrary"), vmem_limit),
        name=name,
    )(*args)


def _cast_mm_kernel(a_ref, w_ref, *rest, epilogue, nk, w_nt):
    if nk > 1:
        acc_ref = rest[-1]
        rest = rest[:-1]
    wb_ref, o_ref = rest[-2:]
    extras = rest[:-2]
    wb = w_ref[...].astype(BF16)
    wb_ref[...] = wb
    part = _dot_nt(a_ref[...], wb) if w_nt else _dot(a_ref[...], wb)

    def finish(acc):
        if epilogue == "plain":
            o_ref[...] = acc.astype(o_ref.dtype)
        elif epilogue == "relu2":
            r = jnp.maximum(acc, 0.0)
            o_ref[...] = (r * r).astype(o_ref.dtype)
        elif epilogue == "resid":
            o_ref[...] = (extras[0][...] + extras[1][...] * acc).astype(o_ref.dtype)
        else:
            raise ValueError(epilogue)

    if nk == 1:
        finish(part)
    else:
        k = pl.program_id(1)

        @pl.when(k == 0)
        def _():
            acc_ref[...] = part

        @pl.when(k > 0)
        def _():
            acc_ref[...] += part

        @pl.when(k == nk - 1)
        def _():
            finish(acc_ref[...])


def _cast_mm(a, w, *, epilogue, out_dtype, tn, tk=None, wt_rows=None, resid=None, gate=None, name="cast_mm"):
    m, kdim = a.shape
    w_nt = wt_rows is not None
    tk = kdim if tk is None else tk
    nk = kdim // tk
    if w_nt:
        r0, n = wt_rows
        assert w.shape[1] == kdim and r0 % 8 == 0 and r0 + pl.cdiv(n, tn) * tn <= w.shape[0]
        w_spec = pl.BlockSpec((pl.Element(tn), pl.Element(tk)),
                              lambda j, k: (pl.multiple_of(r0 + j * tn, 8), k * tk))
        wb_spec = pl.BlockSpec((tn, tk), lambda j, k: (j, k))
        wb_shape = (n, kdim)
    else:
        n = w.shape[1]
        assert w.shape[0] == kdim
        w_spec = wb_spec = pl.BlockSpec((tk, tn), lambda j, k: (k, j))
        wb_shape = (kdim, n)
    assert kdim % tk == 0 and (n % tn == 0 or epilogue == "plain")
    in_specs = [pl.BlockSpec((m, tk), lambda j, k: (0, k)), w_spec]
    args = [a, w]
    if epilogue == "resid":
        in_specs += [pl.BlockSpec((m, tn), lambda j, k: (0, j))] * 2
        args += [resid, gate]
    return pl.pallas_call(
        functools.partial(_cast_mm_kernel, epilogue=epilogue, nk=nk, w_nt=w_nt),
        grid=(pl.cdiv(n, tn), nk),
        in_specs=in_specs,
        out_specs=[wb_spec, pl.BlockSpec((m, tn), lambda j, k: (0, j))],
        out_shape=[jax.ShapeDtypeStruct(wb_shape, BF16),
                   jax.ShapeDtypeStruct((m, n), out_dtype)],
        scratch_shapes=[pltpu.VMEM((m, tn), F32)] if nk > 1 else [],
        compiler_params=_cparams(("arbitrary", "arbitrary")),
        name=name,
    )(*args)


def _qknorm_rows_kernel(x_ref, w_ref, o_ref):
    w = w_ref[...]
    for j in range(x_ref.shape[-1] // LANE):
        blk = x_ref[:, j * LANE:(j + 1) * LANE]
        o_ref[:, j * LANE:(j + 1) * LANE] = blk * lax.rsqrt(jnp.mean(blk * blk, axis=-1, keepdims=True) + EPS) * w


def _qknorm_rows(x, w):
    m, n = x.shape
    return pl.pallas_call(
        _qknorm_rows_kernel,
        grid=(1,),
        in_specs=[pl.BlockSpec((m, n), lambda i: (0, 0)),
                  pl.BlockSpec((1, LANE), lambda i: (0, 0))],
        out_specs=pl.BlockSpec((m, n), lambda i: (0, 0)),
        out_shape=jax.ShapeDtypeStruct((m, n), F32),
        compiler_params=_cparams(("arbitrary",)),
        name="qknorm_rows",
    )(x, w.reshape(1, LANE))


def _gate_kernel(h_ref, wc_ref, wr_ref, oc_ref, or_ref):
    h = h_ref[...]
    oc_ref[...] = _dot(h, wc_ref[...])
    or_ref[...] = _dot_nt(wr_ref[...], h)


def _gate_proj(h, w_gate_cols, tm):
    m, k = h.shape
    wc = jnp.pad(w_gate_cols, ((0, 0), (0, LANE - 8))).astype(BF16)
    wr = jnp.pad(w_gate_cols.T, ((0, 8), (0, 0))).astype(BF16)
    return pl.pallas_call(
        _gate_kernel,
        grid=(m // tm,),
        in_specs=[pl.BlockSpec((tm, k), lambda i: (i, 0)),
                  pl.BlockSpec((k, LANE), lambda i: (0, 0)),
                  pl.BlockSpec((16, k), lambda i: (0, 0))],
        out_specs=[pl.BlockSpec((tm, LANE), lambda i: (i, 0)),
                   pl.BlockSpec((16, tm), lambda i: (0, i))],
        out_shape=[jax.ShapeDtypeStruct((m, LANE), F32),
                   jax.ShapeDtypeStruct((16, m), F32)],
        compiler_params=_cparams(("arbitrary",)),
        name="gate_proj",
    )(h, wc, wr)


def _mlstm_prompt_kernel(proj_ref, gc_ref, gr_ref, bc_ref, br_ref, nw_ref,
                         hm_ref, c_out_ref, n_out_ref, m_out_ref,
                         cext_ref, m_ref, *, dk, dv, nc):
    c_idx = pl.program_id(1)
    L = gc_ref.shape[0]
    nh = M_HEADS
    scale = dk ** -0.5

    @pl.when(c_idx == 0)
    def _():
        cext_ref[...] = jnp.zeros_like(cext_ref)
        m_ref[...] = jnp.zeros_like(m_ref)

    row = lax.broadcasted_iota(jnp.int32, (L, L), 0)
    col = lax.broadcasted_iota(jnp.int32, (L, L), 1)
    causal = row >= col
    tril = causal.astype(F32)

    gcol = gc_ref[...] + bc_ref[...]
    grow = gr_ref[...] + br_ref[...]
    lf_col = _log_sigmoid(gcol)
    lf_row = _log_sigmoid(grow)
    bt_col_all = jnp.dot(tril, lf_col, preferred_element_type=F32, precision=lax.Precision.HIGHEST)
    bt_row_all = lax.dot_general(lf_row, tril, (((1,), (1,)), ((), ())),
                                 preferred_element_type=F32, precision=lax.Precision.HIGHEST)

    lane = lax.broadcasted_iota(jnp.int32, (1, LANE), 1)
    m_new_row = jnp.zeros((1, LANE), F32)
    ones_pad = jnp.ones((L, LANE), BF16)

    for h in range(nh):
        it_c = gcol[:, h:h + 1]
        bt_c = bt_col_all[:, nh + h:nh + h + 1]
        it_r = grow[h:h + 1, :]
        bt_r = bt_row_all[nh + h:nh + h + 1, :]
        m0 = m_ref[h][0:1, 0:1]

        log_d = jnp.where(causal, bt_c - bt_r + it_r, -jnp.inf)
        m_inter = bt_c + m0
        m = jnp.maximum(m_inter, jnp.max(log_d, axis=-1, keepdims=True))
        d_m = jnp.exp(log_d - m)
        g = jnp.exp(m_inter - m)

        q = proj_ref[:, h * dk:(h + 1) * dk].astype(BF16)
        k_f = proj_ref[:, nh * dk + h * dk:nh * dk + (h + 1) * dk] * scale
        v_off = 2 * nh * dk
        v = proj_ref[:, v_off + h * dv:v_off + (h + 1) * dv].astype(BF16)
        vext = jnp.concatenate([v, ones_pad], axis=-1)
        o_off = v_off + nh * dv
        o_gate = proj_ref[:, o_off + h * dv:o_off + (h + 1) * dv]

        s = _dot_nt(q, k_f.astype(BF16)) * d_m
        cext = cext_ref[h]
        numext = _dot(s.astype(BF16), vext) + _dot(q, cext.astype(BF16)) * g
        num = numext[:, :dv]
        den = numext[:, dv:dv + 1]
        den = jnp.maximum(jnp.abs(den), jnp.exp(-m))
        hh = num / den
        y = hh * lax.rsqrt(jnp.mean(hh * hh, axis=-1, keepdims=True) + EPS) * nw_ref[:, h * dv:(h + 1) * dv]
        hm_ref[:, h * dv:(h + 1) * dv] = (_sigmoid(o_gate) * y).astype(hm_ref.dtype)

        m_last = m[L - 1:L, :]
        bt_last = bt_c[L - 1:L, :]
        w_c = jnp.exp(bt_last - bt_c + it_c - m_last)
        g_last = jnp.exp(bt_last + m0 - m_last)
        kw = (k_f * w_c).astype(BF16)
        cext_ref[h] = g_last * cext + _dot_tn(kw, vext)
        m_ref[h] = jnp.broadcast_to(m_last, m_ref.shape[1:])
        m_new_row = jnp.where(lane == h, m_last, m_new_row)

    @pl.when(c_idx == nc - 1)
    def _():
        for h in range(nh):
            c_out_ref[0, h] = cext_ref[h][:, :dv]
            n_out_ref[0, h] = cext_ref[h][:, dv:]
        m_out_ref[0] = m_new_row


def _mlstm_prompt(proj, gcol, grow, b_ig, b_fg, mh_norm_w, batch, seq, dk, dv):
    nh = M_HEADS
    L = min(M_CHUNK, seq)
    nc = seq // L
    m = batch * seq
    width = proj.shape[1]
    bias = jnp.concatenate([b_ig, b_fg]).astype(F32)
    bias_cols = jnp.pad(bias, (0, LANE - 2 * nh)).reshape(1, LANE)
    bias_rows = jnp.pad(bias, (0, 16 - 2 * nh)).reshape(16, 1)
    kern = functools.partial(_mlstm_prompt_kernel, dk=dk, dv=dv, nc=nc)
    return pl.pallas_call(
        kern,
        grid=(batch, nc),
        in_specs=[pl.BlockSpec((L, width), lambda b, c: (b * nc + c, 0)),
                  pl.BlockSpec((L, LANE), lambda b, c: (b * nc + c, 0)),
                  pl.BlockSpec((16, L), lambda b, c: (0, b * nc + c)),
                  pl.BlockSpec((1, LANE), lambda b, c: (0, 0)),
                  pl.BlockSpec((16, 1), lambda b, c: (0, 0)),
                  pl.BlockSpec((1, nh * dv), lambda b, c: (0, 0))],
        out_specs=[pl.BlockSpec((L, nh * dv), lambda b, c: (b * nc + c, 0)),
                   pl.BlockSpec((1, nh, dk, dv), lambda b, c: (b, 0, 0, 0)),
                   pl.BlockSpec((1, nh, dk, LANE), lambda b, c: (b, 0, 0, 0)),
                   pl.BlockSpec((1, 1, LANE), lambda b, c: (b, 0, 0))],
        out_shape=[jax.ShapeDtypeStruct((m, nh * dv), BF16),
                   jax.ShapeDtypeStruct((batch, nh, dk, dv), F32),
                   jax.ShapeDtypeStruct((batch, nh, dk, LANE), F32),
                   jax.ShapeDtypeStruct((batch, 1, LANE), F32)],
        scratch_shapes=[pltpu.VMEM((nh, dk, dv + LANE), F32),
                        pltpu.VMEM((nh, 8, LANE), F32)],
        compiler_params=_cparams(("arbitrary", "arbitrary")),
        name="mlstm_prompt",
    )(proj, gcol, grow, bias_cols, bias_rows, mh_norm_w.reshape(1, nh * dv))


def _mlstm_decode_kernel(proj_ref, g_ref, bias_ref, nw_ref, c0_ref, n0_ref, m0_ref,
                         hm_ref, c_out_ref, n_out_ref, m_out_ref, *, dk, dv):
    nh = M_HEADS
    scale = dk ** -0.5
    gates = g_ref[0] + bias_ref[...]
    lf_all = _log_sigmoid(gates)
    m0_all = m0_ref[0]
    lane = lax.broadcasted_iota(jnp.int32, (1, LANE), 1)
    eye = (lax.broadcasted_iota(jnp.int32, (dk, dk), 0)
           == lax.broadcasted_iota(jnp.int32, (dk, dk), 1))
    m_new_row = jnp.zeros((1, LANE), F32)

    def to_col(r):
        return jnp.sum(jnp.where(eye, jnp.broadcast_to(r, (dk, dk)), 0.0), axis=1, keepdims=True)

    v_off = 2 * nh * dk
    o_off = v_off + nh * dv
    for h in range(nh):
        it = gates[:, h:h + 1]
        lf = lf_all[:, nh + h:nh + h + 1]
        m0 = m0_all[:, h:h + 1]
        q = proj_ref[0, :, h * dk:(h + 1) * dk]
        k = proj_ref[0, :, nh * dk + h * dk:nh * dk + (h + 1) * dk] * scale
        v = proj_ref[0, :, v_off + h * dv:v_off + (h + 1) * dv]
        o_gate = proj_ref[0, :, o_off + h * dv:o_off + (h + 1) * dv]
        c0 = c0_ref[0, h]
        n0 = n0_ref[0, h:h + 1, :]

        m_inter = lf + m0
        m = jnp.maximum(m_inter, it)
        d_m = jnp.exp(it - m)
        g = jnp.exp(m_inter - m)
        s = jnp.sum(q * k, axis=-1, keepdims=True) * d_m
        q_col = to_col(q)
        qc = jnp.sum(q_col * c0, axis=0, keepdims=True)
        num = s * v + qc * g
        den = s + jnp.sum(q * n0, axis=-1, keepdims=True) * g
        den = jnp.maximum(jnp.abs(den), jnp.exp(-m))
        hh = num / den
        y = hh * lax.rsqrt(jnp.mean(hh * hh, axis=-1, keepdims=True) + EPS) * nw_ref[:, h * dv:(h + 1) * dv]
        hm_ref[0, :, h * dv:(h + 1) * dv] = (_sigmoid(o_gate) * y).astype(hm_ref.dtype)

        kw = k * d_m
        c_out_ref[0, h] = g * c0 + to_col(kw) * v
        n_out_ref[0, h:h + 1, :] = g * n0 + kw
        m_new_row = jnp.where(lane == h, m, m_new_row)
    m_out_ref[0] = m_new_row


def _mlstm_decode(proj, gcol, b_ig, b_fg, mh_norm_w, c_all, n_all, m0, layer, dk, dv):
    nh = M_HEADS
    nb, width = proj.shape
    bias = jnp.pad(jnp.concatenate([b_ig, b_fg]).astype(F32), (0, LANE - 2 * nh)).reshape(1, LANE)
    kern = functools.partial(_mlstm_decode_kernel, dk=dk, dv=dv)
    c_flat = c_all.reshape(-1, nh, dk, dv)
    n_flat = n_all.reshape(-1, nh, dk)
    return pl.pallas_call(
        kern,
        grid=(nb,),
        in_specs=[pl.BlockSpec((1, 1, width), lambda b: (b, 0, 0)),
                  pl.BlockSpec((1, 1, LANE), lambda b: (b, 0, 0)),
                  pl.BlockSpec((1, LANE), lambda b: (0, 0)),
                  pl.BlockSpec((1, nh * dv), lambda b: (0, 0)),
                  pl.BlockSpec((1, nh, dk, dv), lambda b: (layer * nb + b, 0, 0, 0)),
                  pl.BlockSpec((1, nh, dk), lambda b: (layer * nb + b, 0, 0)),
                  pl.BlockSpec((1, 1, nh), lambda b: (b, 0, 0))],
        out_specs=[pl.BlockSpec((1, 1, nh * dv), lambda b: (b, 0, 0)),
                   pl.BlockSpec((1, nh, dk, dv), lambda b: (b, 0, 0, 0)),
                   pl.BlockSpec((1, nh, dk), lambda b: (b, 0, 0)),
                   pl.BlockSpec((1, 1, LANE), lambda b: (b, 0, 0))],
        out_shape=[jax.ShapeDtypeStruct((nb, 1, nh * dv), BF16),
                   jax.ShapeDtypeStruct((nb, nh, dk, dv), F32),
                   jax.ShapeDtypeStruct((nb, nh, dk), F32),
                   jax.ShapeDtypeStruct((nb, 1, LANE), F32)],
        compiler_params=_cparams(("arbitrary",)),
        name="mlstm_decode",
    )(proj.reshape(nb, 1, width), gcol.reshape(nb, 1, LANE), bias,
      mh_norm_w.reshape(1, nh * dv), c_flat, n_flat, m0.reshape(nb, 1, nh))


def _lambda_full(lq1, lk1, lq2, lk2, lam_init):
    a = jnp.exp(jnp.sum(lq1 * lk1, axis=-1, keepdims=True))
    b = jnp.exp(jnp.sum(lq2 * lk2, axis=-1, keepdims=True))
    return a - b + lam_init


def _lane_fold(x, op):
    out = x[:, :LANE]
    for i in range(1, x.shape[-1] // LANE):
        out = op(out, x[:, i * LANE:(i + 1) * LANE])
    return out


def _attn_prompt_kernel(q_ref, k_ref, v_ref, lq1_ref, lk1_ref, lq2_ref, lk2_ref, sw_ref,
                        o_ref, kb_ref, vb_ref, s_ref, p_ref, *, dh, lam_init, tq):
    seq = q_ref.shape[0]
    kb_ref[...] = k_ref[...].astype(BF16)
    vb_ref[...] = v_ref[...].astype(BF16)
    lam = _lambda_full(lq1_ref[...], lk1_ref[...], lq2_ref[...], lk2_ref[...], lam_init)
    causal = (lax.broadcasted_iota(jnp.int32, (tq, tq), 0)
              >= lax.broadcasted_iota(jnp.int32, (tq, tq), 1))
    for qi in range(seq // tq):
        kv_len = (qi + 1) * tq
        outs = []
        for c in range(2):
            q_c = q_ref[qi * tq:(qi + 1) * tq, c * dh:(c + 1) * dh]
            mx = None
            for j in range(qi + 1):
                s = _dot_nt(q_c, kb_ref[j * tq:(j + 1) * tq, c * dh:(c + 1) * dh])
                if j == qi:
                    s = jnp.where(causal, s, -jnp.inf)
                s_ref[:, j * tq:(j + 1) * tq] = s
                part = _lane_fold(s, jnp.maximum)
                mx = part if mx is None else jnp.maximum(mx, part)
            m = jnp.max(mx, axis=-1, keepdims=True)
            ls = None
            for j in range(qi + 1):
                p = jnp.exp2(s_ref[:, j * tq:(j + 1) * tq] - m)
                part = _lane_fold(p, jnp.add)
                ls = part if ls is None else ls + part
                p_ref[:, j * tq:(j + 1) * tq] = p.astype(BF16)
            l = jnp.sum(ls, axis=-1, keepdims=True)
            outs.append(_dot(p_ref[:, :kv_len], vb_ref[:kv_len, :]) / l)
        o = outs[0] - lam * outs[1]
        y = o * lax.rsqrt(jnp.mean(o * o, axis=-1, keepdims=True) + EPS) * sw_ref[...]
        o_ref[qi * tq:(qi + 1) * tq, :] = (y * (1.0 - lam_init)).astype(o_ref.dtype)


def _attn_prompt(q, k, v, lams, subln_w, batch, seq, dh, dv, lam_init):
    tq = min(ATTN_TQ, seq)
    m = batch * seq
    kern = functools.partial(_attn_prompt_kernel, dh=dh, lam_init=lam_init, tq=tq)
    lam_spec = pl.BlockSpec((1, dh), lambda b, h: (0, 0))
    return pl.pallas_call(
        kern,
        grid=(batch, A_HEADS),
        in_specs=[pl.BlockSpec((seq, 2 * dh), lambda b, h: (b, h)),
                  pl.BlockSpec((seq, 2 * dh), lambda b, h: (b, h)),
                  pl.BlockSpec((seq, dv), lambda b, h: (b, h)),
                  lam_spec, lam_spec, lam_spec, lam_spec,
                  pl.BlockSpec((1, dv), lambda b, h: (0, 0))],
        out_specs=pl.BlockSpec((seq, dv), lambda b, h: (b, h)),
        out_shape=jax.ShapeDtypeStruct((m, A_HEADS * dv), BF16),
        scratch_shapes=[pltpu.VMEM((seq, 2 * dh), BF16),
                        pltpu.VMEM((seq, dv), BF16),
                        pltpu.VMEM((tq, seq), F32),
                        pltpu.VMEM((tq, seq), BF16)],
        compiler_params=_cparams(("arbitrary", "arbitrary")),
        name="attn_prompt",
    )(q, k, v, *[l.reshape(1, dh) for l in lams], subln_w.reshape(1, dv))


def _attn_decode_kernel(pt_ref, q_ref, kn_ref, vn_ref, *rest, dh, dv, lam_init, n_steps, n_pg, page):
    kp_refs = rest[:n_pg]
    vp_refs = rest[n_pg:2 * n_pg]
    lq1_ref, lk1_ref, lq2_ref, lk2_ref, sw_ref, o_ref, acc_ref, ml_ref = rest[2 * n_pg:]
    step = pl.program_id(1)
    nh = A_HEADS
    nr = 2 * nh
    grp = 2 * LANE
    tok_g = grp // nr
    n_grp = page * nr // grp
    qs = q_ref[0] * (dh ** -0.5 * LOG2E)

    @pl.when(step == 0)
    def _():
        acc_ref[...] = jnp.zeros_like(acc_ref)
        ml_ref[0] = jnp.full(ml_ref.shape[1:], -jnp.inf, F32)
        ml_ref[1] = jnp.zeros(ml_ref.shape[1:], F32)

    def rescale(m_blk):
        m_old = ml_ref[0][:, 0:1]
        m_new = jnp.maximum(m_old, m_blk)
        return m_new, jnp.exp2(m_old - m_new)

    def commit(m_new, alpha, l_blk, pv):
        acc_ref[...] = alpha * acc_ref[...] + pv
        ml_ref[1] = jnp.broadcast_to(alpha * ml_ref[1][:, 0:1] + l_blk, ml_ref.shape[1:])
        ml_ref[0] = jnp.broadcast_to(m_new, ml_ref.shape[1:])

    shape3 = (n_pg * n_grp, nr, grp)
    r3 = lax.broadcasted_iota(jnp.int32, shape3, 1)
    c3 = lax.broadcasted_iota(jnp.int32, shape3, 2)
    valid = (c3 % nr) == (r3 % nh) * 2 + r3 // nh
    e_row = lax.broadcasted_iota(jnp.int32, (grp, LANE), 0)
    e_col = lax.broadcasted_iota(jnp.int32, (grp, LANE), 1)
    fold = ((e_row // nr == e_col // nh) & ((e_row % nr) // 2 == e_col % nh)).astype(BF16)

    qb = qs.astype(BF16)
    pieces = []
    for pg in range(n_pg):
        s_pg = _dot_nt(qb, kp_refs[pg][...].astype(BF16))
        pieces += [s_pg[:, g * grp:(g + 1) * grp] for g in range(n_grp)]
    s3 = jnp.where(valid, jnp.concatenate(pieces, axis=0).reshape(shape3), -jnp.inf)
    m_new, alpha = rescale(jnp.max(jnp.max(s3, axis=0), axis=-1, keepdims=True))
    p3 = jnp.exp2(s3 - m_new[None])
    l_blk = jnp.sum(jnp.sum(p3, axis=0), axis=-1, keepdims=True)
    p2 = _dot(p3.reshape(n_pg * n_grp * nr, grp).astype(BF16), fold).astype(BF16)
    pv = None
    for pg in range(n_pg):
        lhs = jnp.concatenate([p2[(pg * n_grp + g) * nr:(pg * n_grp + g + 1) * nr] for g in range(n_grp)], axis=1)
        part = _dot(lhs, vp_refs[pg][...].astype(BF16))
        pv = part if pv is None else pv + part
    commit(m_new, alpha, l_blk, pv)

    @pl.when(step == n_steps - 1)
    def _():
        s_new = jnp.sum(qs * kn_ref[0], axis=-1, keepdims=True)
        m_fin, a_fin = rescale(s_new)
        p_new = jnp.exp2(s_new - m_fin)
        vn = vn_ref[0]
        commit(m_fin, a_fin, p_new, p_new * jnp.concatenate([vn, vn], axis=0))
        lam = _lambda_full(lq1_ref[...], lk1_ref[...], lq2_ref[...], lk2_ref[...], lam_init)
        o_n = acc_ref[...] / ml_ref[1][:, 0:1]
        o = o_n[:nh] - lam * o_n[nh:]
        y = o * lax.rsqrt(jnp.mean(o * o, axis=-1, keepdims=True) + EPS) * sw_ref[...]
        o_ref[0] = (y * (1.0 - lam_init)).astype(o_ref.dtype)


def _attn_decode(page_ids, q, k_new, v_new, cache_k, cache_v, lams, subln_w, dh, dv, lam_init):
    nb, n_pages = page_ids.shape
    page = cache_k.shape[-4]
    nh = A_HEADS
    nr = 2 * nh
    n_pg = math.gcd(DECODE_PAGES, n_pages)
    n_steps = n_pages // n_pg
    k2 = cache_k.reshape(-1, dh)
    v2 = cache_v.reshape(-1, dv)

    def rows_ch(x):
        return x.reshape(nb, nh, 2, dh).transpose(0, 2, 1, 3).reshape(nb, nr, dh)

    kern = functools.partial(_attn_decode_kernel, dh=dh, dv=dv, lam_init=lam_init,
                             n_steps=n_steps, n_pg=n_pg, page=page)
    lam_spec = pl.BlockSpec((1, dh), lambda b, s, pt: (0, 0))
    k_specs = [pl.BlockSpec((page * nr, dh), functools.partial(lambda b, s, pt, g: (pt[b, s * n_pg + g], 0), g=g))
               for g in range(n_pg)]
    v_specs = [pl.BlockSpec((page * nh, dv), functools.partial(lambda b, s, pt, g: (pt[b, s * n_pg + g], 0), g=g))
               for g in range(n_pg)]
    grid_spec = pltpu.PrefetchScalarGridSpec(
        num_scalar_prefetch=1,
        grid=(nb, n_steps),
        in_specs=[pl.BlockSpec((1, nr, dh), lambda b, s, pt: (b, 0, 0)),
                  pl.BlockSpec((1, nr, dh), lambda b, s, pt: (b, 0, 0)),
                  pl.BlockSpec((1, nh, dv), lambda b, s, pt: (b, 0, 0)),
                  *k_specs, *v_specs,
                  lam_spec, lam_spec, lam_spec, lam_spec,
                  pl.BlockSpec((1, dv), lambda b, s, pt: (0, 0))],
        out_specs=pl.BlockSpec((1, nh, dv), lambda b, s, pt: (b, 0, 0)),
        scratch_shapes=[pltpu.VMEM((nr, dv), F32),
                        pltpu.VMEM((2, nr, LANE), F32)],
    )
    out = pl.pallas_call(
        kern,
        grid_spec=grid_spec,
        out_shape=jax.ShapeDtypeStruct((nb, nh, dv), BF16),
        compiler_params=_cparams(("arbitrary", "arbitrary")),
        name="attn_decode",
    )(page_ids, rows_ch(q), rows_ch(k_new), v_new.reshape(nb, nh, dv),
      *([k2] * n_pg), *([v2] * n_pg), *[l.reshape(1, dh) for l in lams], subln_w.reshape(1, dv))
    return out.reshape(nb, nh * dv)


def _pick(m, pref):
    return pref if m % pref == 0 else m


def _row_tiles(m, rows_per_mod):
    unit = m if rows_per_mod == 1 else rows_per_mod
    return _pick(unit, 1024), _pick(unit, 512)


def _in_proj(x, sc1, sh1, rows_per_mod, w, q_dtype, q_scale, tn, k_rows_out):
    norm1_w, wt_m, wt_a, w_g, (n_m, aq, av), q_norm_w, k_norm_w = w
    m = x.shape[0]
    tm, tr = _row_tiles(m, rows_per_mod)
    h = _norm_mod(x, norm1_w, sc1, sh1, rows_per_mod, tr)
    proj_m = _mm([(h, wt_m)], wt_row0=0, n_out=n_m, tm=tm, tn=tn, name="in_mlstm")
    gcol, grow = _gate_proj(h, w_g, _pick(m, 512))
    qa = _mm([(h, wt_a)], wt_row0=0, n_out=aq, epilogue="qknorm", normw=q_norm_w, post_scale=q_scale,
             out_dtype=q_dtype, tm=tm, tn=tn, name="in_aq")
    tm_k = _pick(tm, 512) if k_rows_out else tm
    ka = _mm([(h, wt_a)], wt_row0=aq, n_out=aq, epilogue="qknorm", normw=k_norm_w,
             rows_out=k_rows_out, tm=tm_k, tn=tn, name="in_ak")
    va = _mm([(h, wt_a)], wt_row0=2 * aq, n_out=av, tm=tm, tn=tn, name="in_av")
    return proj_m, gcol, grow, qa, ka, va


def kernel(x_prompt, x_sample, cache_k, cache_v, state_C, state_n, state_m, page_table,
           c_prompt, c_sample, w_ada, b_ada, norm1_w, w_in, b_igate, b_fgate, mh_norm_w,
           q_norm_w, k_norm_w, lambda_q1, lambda_k1, lambda_q2, lambda_k2, subln_w,
           w_out, norm2_w, w_up, w_down):
    depth = w_ada.shape[0]
    batch, seq, d = x_prompt.shape
    nb = x_sample.shape[0]
    dk = state_C.shape[3]
    dv = state_C.shape[4]
    dh = cache_k.shape[-1]
    adv = cache_v.shape[-1]
    mq, mv = M_HEADS * dk, M_HEADS * dv
    aq, av = A_HEADS * 2 * dh, A_HEADS * adv
    n_pool = cache_k.shape[1]
    n_m = 2 * mq + 2 * mv
    o_attn = n_m + 2 * M_HEADS

    yp = x_prompt.reshape(batch * seq, d)
    ys = x_sample.reshape(nb, d)
    outs = [[] for _ in range(10)]
    for l in range(depth):
        lam_init = 0.8 - 0.6 * math.exp(-0.3 * l)
        lams = (lambda_q1[l], lambda_k1[l], lambda_q2[l], lambda_k2[l])

        c_all = jnp.concatenate([c_prompt, c_sample], axis=0)
        pad = (-c_all.shape[0]) % 8
        mod = _ada(jnp.pad(c_all, ((0, pad), (0, 0))), w_ada[l], b_ada[l])
        mods_p = tuple(mod[:batch, i * d:(i + 1) * d] for i in range(N_MOD))
        mods_s = tuple(mod[batch:batch + nb, i * d:(i + 1) * d] for i in range(N_MOD))

        sh1, sc1, g1, sh2, sc2, g2 = mods_s
        tr_s = _row_tiles(nb, 1)[1]
        h = _norm_mod(ys, norm1_w[l], sc1, sh1, 1, tr_s)
        w_in_t = jnp.swapaxes(w_in, 1, 2)[l]
        wt_m, proj = _cast_mm(h, w_in_t, epilogue="plain", out_dtype=F32, tn=512,
                              wt_rows=(0, o_attn), name="in_proj_m_s")
        wt_a, proj_a = _cast_mm(h, w_in_t, epilogue="plain", out_dtype=F32, tn=512,
                                wt_rows=(o_attn, 2 * aq + av), name="in_proj_a_s")
        proj_m = proj[:, :n_m]
        gcol = jnp.pad(proj[:, n_m:o_attn], ((0, 0), (0, LANE - 2 * M_HEADS)))
        qa = _qknorm_rows(proj_a[:, :aq], q_norm_w[l])
        ks = _qknorm_rows(proj_a[:, aq:2 * aq], k_norm_w[l])
        vs = proj_a[:, 2 * aq:2 * aq + av]
        hm, cs, ns, ms = _mlstm_decode(proj_m, gcol, b_igate[l], b_fgate[l], mh_norm_w[l],
                                       state_C, state_n, state_m[l], l, dk, dv)
        oa = _attn_decode(page_table + l * n_pool, qa, ks, vs, cache_k, cache_v,
                          lams, subln_w[l], dh, adv, lam_init)
        mix = jnp.concatenate([hm.reshape(nb, mv), oa], axis=1)
        w_out_b, x1 = _cast_mm(mix, w_out[l], epilogue="resid", out_dtype=F32, tn=512,
                               resid=ys, gate=g1, name="out_proj_s")
        h2 = _norm_mod(x1, norm2_w[l], sc2, sh2, 1, tr_s)
        w_up_b, u = _cast_mm(h2, w_up[l], epilogue="relu2", out_dtype=BF16, tn=512, name="ffn_up_s")
        w_down_b, ys = _cast_mm(u, w_down[l], epilogue="resid", out_dtype=F32, tn=_pick(d, 1024), tk=2048,
                                resid=x1, gate=g2, name="ffn_down_s")
        ms = ms[:, 0, :M_HEADS]

        sh1, sc1, g1, sh2, sc2, g2 = mods_p
        tm, tr = _row_tiles(batch * seq, seq)
        w_proj = (norm1_w[l], wt_m, wt_a, w_in[l][:, n_m:o_attn], (n_m, aq, av), q_norm_w[l], k_norm_w[l])
        proj_m, gcol, grow, qa, (kp, kp_rows), vp = _in_proj(yp, sc1, sh1, seq, w_proj, BF16,
                                                             dh ** -0.5 * LOG2E, 1024, True)
        hm, cp, np_, mp = _mlstm_prompt(proj_m, gcol, grow, b_igate[l], b_fgate[l],
                                        mh_norm_w[l], batch, seq, dk, dv)
        oa = _attn_prompt(qa, kp, vp, lams, subln_w[l], batch, seq, dh, adv, lam_init)
        x1 = _mm([(hm, w_out_b), (oa, w_out_b)], w_row0=[0, mv], epilogue="resid", resid=yp, gate=g1,
                 rows_per_gate=seq, tm=tm, tn=1024, name="out_proj")
        h2 = _norm_mod(x1, norm2_w[l], sc2, sh2, seq, tr)
        u = _mm([(h2, w_up_b)], epilogue="relu2", out_dtype=BF16, tm=tm, tn=1024, name="ffn_up")
        yp = _mm([(u, w_down_b)], epilogue="resid", resid=x1, gate=g2, rows_per_gate=seq,
                 tm=tm, tn=_pick(d, 1024), tk=4096, vmem_limit=VMEM_LIMIT_BIG, name="ffn_down")

        for lst, val in zip(outs, (kp_rows.reshape(batch, seq, A_HEADS, 2, dh),
                                   vp.reshape(batch, seq, A_HEADS, adv),
                                   cp, np_[..., 0], mp[:, 0, :M_HEADS],
                                   ks.reshape(nb, 1, A_HEADS, 2, dh), vs.reshape(nb, 1, A_HEADS, adv),
                                   cs, ns, ms)):
            lst.append(val)
    stacked = [jnp.stack(o) for o in outs]
    return (yp.reshape(batch, seq, d), ys.reshape(nb, 1, d), *stacked)
```

```python
import functools
import math

import jax
import jax.numpy as jnp
from jax import lax
from jax.experimental import pallas as pl
from jax.experimental.pallas import tpu as pltpu

F32 = jnp.float32
BF16 = jnp.bfloat16
EPS = 1e-6
LOG2E = math.log2(math.e)
LANE = 128
VMEM_LIMIT = 56 * 1024 * 1024
VMEM_LIMIT_BIG = 60 * 1024 * 1024

M_HEADS = 4
A_HEADS = 8
N_MOD = 6
M_CHUNK = 256
ATTN_TQ = 512
DECODE_PAGES = 8


def _cparams(sem, vmem_limit=VMEM_LIMIT):
    return pltpu.CompilerParams(dimension_semantics=sem, vmem_limit_bytes=vmem_limit)


def _dot(a, b):
    return jnp.dot(a, b, preferred_element_type=F32)


def _dot_nt(a, b):
    return lax.dot_general(a, b, (((1,), (1,)), ((), ())), preferred_element_type=F32)


def _dot_tn(a, b):
    return lax.dot_general(a, b, (((0,), (0,)), ((), ())), preferred_element_type=F32)


def _log_sigmoid(x):
    return jnp.minimum(x, 0.0) - jnp.log(1.0 + jnp.exp(-jnp.abs(x)))


def _sigmoid(x):
    return 1.0 / (1.0 + jnp.exp(-x))


def _ada_kernel(c_ref, w_ref, b_ref, o_ref):
    c = c_ref[...]
    a = (c * _sigmoid(c)).astype(BF16)
    o_ref[...] = _dot(a, w_ref[...].astype(BF16)) + b_ref[...]


def _ada(c, w, b, tn=512):
    m, k = c.shape
    n = w.shape[1]
    return pl.pallas_call(
        _ada_kernel,
        grid=(n // tn,),
        in_specs=[pl.BlockSpec((m, k), lambda j: (0, 0)),
                  pl.BlockSpec((k, tn), lambda j: (0, j)),
                  pl.BlockSpec((1, tn), lambda j: (0, j))],
        out_specs=pl.BlockSpec((m, tn), lambda j: (0, j)),
        out_shape=jax.ShapeDtypeStruct((m, n), F32),
        compiler_params=_cparams(("arbitrary",)),
        name="ada",
    )(c, w, b.reshape(1, n))


def _norm_mod_kernel(x_ref, w_ref, sc_ref, sh_ref, o_ref):
    x = x_ref[...]
    sc = sc_ref[...].reshape(-1, x.shape[-1])
    sh = sh_ref[...].reshape(-1, x.shape[-1])
    y = x * lax.rsqrt(jnp.mean(x * x, axis=-1, keepdims=True) + EPS) * w_ref[...]
    o_ref[...] = (y * (1.0 + sc) + sh).astype(o_ref.dtype)


def _norm_mod(x, w, sc, sh, rows_per_mod, tr):
    m, d = x.shape
    if rows_per_mod == 1:
        mod_spec = pl.BlockSpec((tr, d), lambda i: (i, 0))
    else:
        per = rows_per_mod // tr
        sc = sc.reshape(-1, 1, d)
        sh = sh.reshape(-1, 1, d)
        mod_spec = pl.BlockSpec((1, 1, d), lambda i: (i // per, 0, 0))
    return pl.pallas_call(
        _norm_mod_kernel,
        grid=(m // tr,),
        in_specs=[pl.BlockSpec((tr, d), lambda i: (i, 0)),
                  pl.BlockSpec((1, d), lambda i: (0, 0)),
                  mod_spec, mod_spec],
        out_specs=pl.BlockSpec((tr, d), lambda i: (i, 0)),
        out_shape=jax.ShapeDtypeStruct((m, d), BF16),
        compiler_params=_cparams(("arbitrary",)),
        name="norm_mod",
    )(x, w.reshape(1, d), sc, sh)


def _mm_kernel(*refs, n_pairs, epilogue, nk, post_scale, rows_out, w_nt):
    ab = refs[:2 * n_pairs]
    rest = refs[2 * n_pairs:]
    if rows_out:
        o3_ref = rest[-1]
        rest = rest[:-1]
    o_ref = rest[-1]
    extras = rest[:-1]

    def partial_product():
        dot = _dot_nt if w_nt else _dot
        r = dot(ab[0][...], ab[1][...])
        for p in range(1, n_pairs):
            r = r + dot(ab[2 * p][...], ab[2 * p + 1][...])
        return r

    def finish(acc):
        if epilogue == "plain":
            o_ref[...] = acc.astype(o_ref.dtype)
        elif epilogue == "relu2":
            r = jnp.maximum(acc, 0.0)
            o_ref[...] = (r * r).astype(o_ref.dtype)
        elif epilogue == "qknorm":
            w = extras[0][...]
            for j in range(acc.shape[-1] // LANE):
                blk = acc[:, j * LANE:(j + 1) * LANE]
                y = blk * lax.rsqrt(jnp.mean(blk * blk, axis=-1, keepdims=True) + EPS) * w
                if post_scale != 1.0:
                    y = y * post_scale
                o_ref[:, j * LANE:(j + 1) * LANE] = y.astype(o_ref.dtype)
                if rows_out:
                    o3_ref[:, j, :] = y.astype(o3_ref.dtype)
        elif epilogue == "resid":
            x = extras[0][...]
            g = extras[1][...].reshape(-1, acc.shape[-1])
            o_ref[...] = (x + g * acc).astype(o_ref.dtype)
        else:
            raise ValueError(epilogue)

    if nk == 1:
        finish(partial_product())
    else:
        k = pl.program_id(2)

        @pl.when(k == 0)
        def _():
            o_ref[...] = partial_product()

        @pl.when((k > 0) & (k < nk - 1))
        def _():
            o_ref[...] += partial_product()

        @pl.when(k == nk - 1)
        def _():
            finish(o_ref[...] + partial_product())


def _mm(pairs, *, epilogue="plain", out_dtype=F32, tm, tn, tk=None, n_out=None, w_row0=None, wt_row0=None,
        normw=None, post_scale=1.0, rows_out=False, resid=None, gate=None, rows_per_gate=1,
        vmem_limit=VMEM_LIMIT, name="mm"):
    m, kdim = pairs[0][0].shape
    w_nt = wt_row0 is not None
    n = n_out if (n_out is not None) else pairs[0][1].shape[1]
    tk = kdim if tk is None else tk
    nk = kdim // tk
    w_row0 = [0] * len(pairs) if w_row0 is None else w_row0
    assert m % tm == 0 and n % tn == 0 and kdim % tk == 0
    grid = (m // tm, n // tn, nk)
    in_specs, args = [], []
    for (a, w), r0 in zip(pairs, w_row0):
        in_specs.append(pl.BlockSpec((tm, tk), lambda i, j, k: (i, k)))
        if w_nt:
            assert len(pairs) == 1 and a.shape == (m, kdim) and w.shape[1] == kdim
            assert wt_row0 % tn == 0 and w.shape[0] >= wt_row0 + n
            in_specs.append(pl.BlockSpec((tn, tk), lambda i, j, k: (wt_row0 // tn + j, k)))
        else:
            assert a.shape == (m, kdim) and w.shape[0] >= r0 + kdim and w.shape[1] >= n and r0 % tk == 0
            in_specs.append(pl.BlockSpec((tk, tn), functools.partial(lambda i, j, k, kb: (k + kb, j), kb=r0 // tk)))
        args += [a, w]
    if epilogue == "qknorm":
        in_specs.append(pl.BlockSpec((1, LANE), lambda i, j, k: (0, 0)))
        args.append(normw.reshape(1, LANE))
    if epilogue == "resid":
        in_specs.append(pl.BlockSpec((tm, tn), lambda i, j, k: (i, j)))
        args.append(resid)
        if rows_per_gate == 1:
            in_specs.append(pl.BlockSpec((tm, tn), lambda i, j, k: (i, j)))
            args.append(gate)
        else:
            per = rows_per_gate // tm
            in_specs.append(pl.BlockSpec((1, 1, tn), lambda i, j, k: (i // per, 0, j)))
            args.append(gate.reshape(-1, 1, n))
    assert nk == 1 or out_dtype == F32
    out_specs = pl.BlockSpec((tm, tn), lambda i, j, k: (i, j))
    out_shape = jax.ShapeDtypeStruct((m, n), out_dtype)
    if rows_out:
        assert epilogue == "qknorm" and (tn // LANE) % 8 == 0
        out_specs = [out_specs, pl.BlockSpec((tm, tn // LANE, LANE), lambda i, j, k: (i, j, 0))]
        out_shape = [out_shape, jax.ShapeDtypeStruct((m, n // LANE, LANE), out_dtype)]
    return pl.pallas_call(
        functools.partial(_mm_kernel, n_pairs=len(pairs), epilogue=epilogue, nk=nk, post_scale=post_scale,
                          rows_out=rows_out, w_nt=w_nt),
        grid=grid,
        in_specs=in_specs,
        out_specs=out_specs,
        out_shape=out_shape,
        compiler_params=_cparams(("arbitrary", "arbitrary", "arbitrary"), vmem_limit),
        name=name,
    )(*args)


def _cast_mm_kernel(a_ref, w_ref, *rest, epilogue, nk, w_nt):
    if nk > 1:
        acc_ref = rest[-1]
        rest = rest[:-1]
    wb_ref, o_ref = rest[-2:]
    extras = rest[:-2]
    wb = w_ref[...].astype(BF16)
    wb_ref[...] = wb
    part = _dot_nt(a_ref[...], wb) if w_nt else _dot(a_ref[...], wb)

    def finish(acc):
        if epilogue == "plain":
            o_ref[...] = acc.astype(o_ref.dtype)
        elif epilogue == "relu2":
            r = jnp.maximum(acc, 0.0)
            o_ref[...] = (r * r).astype(o_ref.dtype)
        elif epilogue == "resid":
            o_ref[...] = (extras[0][...] + extras[1][...] * acc).astype(o_ref.dtype)
        else:
            raise ValueError(epilogue)

    if nk == 1:
        finish(part)
    else:
        k = pl.program_id(1)

        @pl.when(k == 0)
        def _():
            acc_ref[...] = part

        @pl.when(k > 0)
        def _():
            acc_ref[...] += part

        @pl.when(k == nk - 1)
        def _():
            finish(acc_ref[...])


def _cast_mm(a, w, *, epilogue, out_dtype, tn, tk=None, wt_rows=None, resid=None, gate=None, name="cast_mm"):
    m, kdim = a.shape
    w_nt = wt_rows is not None
    tk = kdim if tk is None else tk
    nk = kdim // tk
    if w_nt:
        r0, n = wt_rows
        assert w.shape[1] == kdim and r0 % 8 == 0 and r0 + pl.cdiv(n, tn) * tn <= w.shape[0]
        w_spec = pl.BlockSpec((pl.Element(tn), pl.Element(tk)),
                              lambda j, k: (pl.multiple_of(r0 + j * tn, 8), k * tk))
        wb_spec = pl.BlockSpec((tn, tk), lambda j, k: (j, k))
        wb_shape = (n, kdim)
    else:
        n = w.shape[1]
        assert w.shape[0] == kdim
        w_spec = wb_spec = pl.BlockSpec((tk, tn), lambda j, k: (k, j))
        wb_shape = (kdim, n)
    assert kdim % tk == 0 and (n % tn == 0 or epilogue == "plain")
    in_specs = [pl.BlockSpec((m, tk), lambda j, k: (0, k)), w_spec]
    args = [a, w]
    if epilogue == "resid":
        in_specs += [pl.BlockSpec((m, tn), lambda j, k: (0, j))] * 2
        args += [resid, gate]
    return pl.pallas_call(
        functools.partial(_cast_mm_kernel, epilogue=epilogue, nk=nk, w_nt=w_nt),
        grid=(pl.cdiv(n, tn), nk),
        in_specs=in_specs,
        out_specs=[wb_spec, pl.BlockSpec((m, tn), lambda j, k: (0, j))],
        out_shape=[jax.ShapeDtypeStruct(wb_shape, BF16),
                   jax.ShapeDtypeStruct((m, n), out_dtype)],
        scratch_shapes=[pltpu.VMEM((m, tn), F32)] if nk > 1 else [],
        compiler_params=_cparams(("arbitrary", "arbitrary")),
        name=name,
    )(*args)


def _qknorm_rows_kernel(x_ref, w_ref, o_ref):
    w = w_ref[...]
    for j in range(x_ref.shape[-1] // LANE):
        blk = x_ref[:, j * LANE:(j + 1) * LANE]
        o_ref[:, j * LANE:(j + 1) * LANE] = blk * lax.rsqrt(jnp.mean(blk * blk, axis=-1, keepdims=True) + EPS) * w


def _qknorm_rows(x, w):
    m, n = x.shape
    return pl.pallas_call(
        _qknorm_rows_kernel,
        grid=(1,),
        in_specs=[pl.BlockSpec((m, n), lambda i: (0, 0)),
                  pl.BlockSpec((1, LANE), lambda i: (0, 0))],
        out_specs=pl.BlockSpec((m, n), lambda i: (0, 0)),
        out_shape=jax.ShapeDtypeStruct((m, n), F32),
        compiler_params=_cparams(("arbitrary",)),
        name="qknorm_rows",
    )(x, w.reshape(1, LANE))


def _gate_kernel(h_ref, wc_ref, oc_ref):
    oc_ref[...] = _dot(h_ref[...], wc_ref[...])


def _gate_proj(h, w_gate_cols, tm):
    m, k = h.shape
    wc = jnp.pad(w_gate_cols, ((0, 0), (0, LANE - 8))).astype(BF16)
    return pl.pallas_call(
        _gate_kernel,
        grid=(m // tm,),
        in_specs=[pl.BlockSpec((tm, k), lambda i: (i, 0)),
                  pl.BlockSpec((k, LANE), lambda i: (0, 0))],
        out_specs=pl.BlockSpec((tm, LANE), lambda i: (i, 0)),
        out_shape=jax.ShapeDtypeStruct((m, LANE), F32),
        compiler_params=_cparams(("arbitrary",)),
        name="gate_proj",
    )(h, wc)


def _mlstm_prompt_kernel(proj_ref, gc_ref, bc_ref, nw_ref,
                         hm_ref, c_out_ref, n_out_ref, m_out_ref,
                         cext_ref, m_ref, *, dk, dv, nc):
    c_idx = pl.program_id(1)
    L = gc_ref.shape[0]
    nh = M_HEADS
    scale = dk ** -0.5

    @pl.when(c_idx == 0)
    def _():
        cext_ref[...] = jnp.zeros_like(cext_ref)
        m_ref[...] = jnp.zeros_like(m_ref)

    row = lax.broadcasted_iota(jnp.int32, (L, L), 0)
    col = lax.broadcasted_iota(jnp.int32, (L, L), 1)
    causal = row >= col
    tril = causal.astype(F32)

    gcol = gc_ref[...] + bc_ref[...]
    grow = gcol.T[:16, :]
    lf_col = _log_sigmoid(gcol)
    lf_row = _log_sigmoid(grow)
    bt_col_all = jnp.dot(tril, lf_col, preferred_element_type=F32, precision=lax.Precision.HIGHEST)
    bt_row_all = lax.dot_general(lf_row, tril, (((1,), (1,)), ((), ())),
                                 preferred_element_type=F32, precision=lax.Precision.HIGHEST)

    lane = lax.broadcasted_iota(jnp.int32, (1, LANE), 1)
    m_new_row = jnp.zeros((1, LANE), F32)
    ones_pad = jnp.ones((L, LANE), BF16)

    for h in range(nh):
        it_c = gcol[:, h:h + 1]
        bt_c = bt_col_all[:, nh + h:nh + h + 1]
        it_r = grow[h:h + 1, :]
        bt_r = bt_row_all[nh + h:nh + h + 1, :]
        m0 = m_ref[h][0:1, 0:1]

        log_d = jnp.where(causal, bt_c - bt_r + it_r, -jnp.inf)
        m_inter = bt_c + m0
        m = jnp.maximum(m_inter, jnp.max(log_d, axis=-1, keepdims=True))
        d_m = jnp.exp(log_d - m)
        g = jnp.exp(m_inter - m)

        q = proj_ref[:, h * dk:(h + 1) * dk].astype(BF16)
        k_f = proj_ref[:, nh * dk + h * dk:nh * dk + (h + 1) * dk] * scale
        v_off = 2 * nh * dk
        v = proj_ref[:, v_off + h * dv:v_off + (h + 1) * dv].astype(BF16)
        vext = jnp.concatenate([v, ones_pad], axis=-1)
        o_off = v_off + nh * dv
        o_gate = proj_ref[:, o_off + h * dv:o_off + (h + 1) * dv]

        s = _dot_nt(q, k_f.astype(BF16)) * d_m
        cext = cext_ref[h]
        numext = _dot(s.astype(BF16), vext) + _dot(q, cext.astype(BF16)) * g
        num = numext[:, :dv]
        den = numext[:, dv:dv + 1]
        den = jnp.maximum(jnp.abs(den), jnp.exp(-m))
        hh = num / den
        y = hh * lax.rsqrt(jnp.mean(hh * hh, axis=-1, keepdims=True) + EPS) * nw_ref[:, h * dv:(h + 1) * dv]
        hm_ref[:, h * dv:(h + 1) * dv] = (_sigmoid(o_gate) * y).astype(hm_ref.dtype)

        m_last = m[L - 1:L, :]
        bt_last = bt_c[L - 1:L, :]
        w_c = jnp.exp(bt_last - bt_c + it_c - m_last)
        g_last = jnp.exp(bt_last + m0 - m_last)
        kw = (k_f * w_c).astype(BF16)
        cext_ref[h] = g_last * cext + _dot_tn(kw, vext)
        m_ref[h] = jnp.broadcast_to(m_last, m_ref.shape[1:])
        m_new_row = jnp.where(lane == h, m_last, m_new_row)

    @pl.when(c_idx == nc - 1)
    def _():
        for h in range(nh):
            c_out_ref[0, h] = cext_ref[h][:, :dv]
            n_out_ref[0, h] = cext_ref[h][:, dv:]
        m_out_ref[0] = m_new_row


def _mlstm_prompt(proj, gcol, b_ig, b_fg, mh_norm_w, batch, seq, dk, dv):
    nh = M_HEADS
    L = min(M_CHUNK, seq)
    nc = seq // L
    m = batch * seq
    width = proj.shape[1]
    bias = jnp.concatenate([b_ig, b_fg]).astype(F32)
    bias_cols = jnp.pad(bias, (0, LANE - 2 * nh)).reshape(1, LANE)
    kern = functools.partial(_mlstm_prompt_kernel, dk=dk, dv=dv, nc=nc)
    return pl.pallas_call(
        kern,
        grid=(batch, nc),
        in_specs=[pl.BlockSpec((L, width), lambda b, c: (b * nc + c, 0)),
                  pl.BlockSpec((L, LANE), lambda b, c: (b * nc + c, 0)),
                  pl.BlockSpec((1, LANE), lambda b, c: (0, 0)),
                  pl.BlockSpec((1, nh * dv), lambda b, c: (0, 0))],
        out_specs=[pl.BlockSpec((L, nh * dv), lambda b, c: (b * nc + c, 0)),
                   pl.BlockSpec((1, nh, dk, dv), lambda b, c: (b, 0, 0, 0)),
                   pl.BlockSpec((1, nh, dk, LANE), lambda b, c: (b, 0, 0, 0)),
                   pl.BlockSpec((1, 1, LANE), lambda b, c: (b, 0, 0))],
        out_shape=[jax.ShapeDtypeStruct((m, nh * dv), BF16),
                   jax.ShapeDtypeStruct((batch, nh, dk, dv), F32),
                   jax.ShapeDtypeStruct((batch, nh, dk, LANE), F32),
                   jax.ShapeDtypeStruct((batch, 1, LANE), F32)],
        scratch_shapes=[pltpu.VMEM((nh, dk, dv + LANE), F32),
                        pltpu.VMEM((nh, 8, LANE), F32)],
        compiler_params=_cparams(("arbitrary", "arbitrary")),
        name="mlstm_prompt",
    )(proj, gcol, bias_cols, mh_norm_w.reshape(1, nh * dv))


def _mlstm_decode_kernel(proj_ref, g_ref, bias_ref, nw_ref, c0_ref, n0_ref, m0_ref,
                         hm_ref, c_out_ref, n_out_ref, m_out_ref, *, dk, dv):
    nh = M_HEADS
    scale = dk ** -0.5
    gates = g_ref[0] + bias_ref[...]
    lf_all = _log_sigmoid(gates)
    m0_all = m0_ref[0]
    lane = lax.broadcasted_iota(jnp.int32, (1, LANE), 1)
    eye = (lax.broadcasted_iota(jnp.int32, (dk, dk), 0)
           == lax.broadcasted_iota(jnp.int32, (dk, dk), 1))
    m_new_row = jnp.zeros((1, LANE), F32)

    def to_col(r):
        return jnp.sum(jnp.where(eye, jnp.broadcast_to(r, (dk, dk)), 0.0), axis=1, keepdims=True)

    v_off = 2 * nh * dk
    o_off = v_off + nh * dv
    for h in range(nh):
        it = gates[:, h:h + 1]
        lf = lf_all[:, nh + h:nh + h + 1]
        m0 = m0_all[:, h:h + 1]
        q = proj_ref[0, :, h * dk:(h + 1) * dk]
        k = proj_ref[0, :, nh * dk + h * dk:nh * dk + (h + 1) * dk] * scale
        v = proj_ref[0, :, v_off + h * dv:v_off + (h + 1) * dv]
        o_gate = proj_ref[0, :, o_off + h * dv:o_off + (h + 1) * dv]
        c0 = c0_ref[0, h]
        n0 = n0_ref[0, h:h + 1, :]

        m_inter = lf + m0
        m = jnp.maximum(m_inter, it)
        d_m = jnp.exp(it - m)
        g = jnp.exp(m_inter - m)
        s = jnp.sum(q * k, axis=-1, keepdims=True) * d_m
        q_col = to_col(q)
        qc = jnp.sum(q_col * c0, axis=0, keepdims=True)
        num = s * v + qc * g
        den = s + jnp.sum(q * n0, axis=-1, keepdims=True) * g
        den = jnp.maximum(jnp.abs(den), jnp.exp(-m))
        hh = num / den
        y = hh * lax.rsqrt(jnp.mean(hh * hh, axis=-1, keepdims=True) + EPS) * nw_ref[:, h * dv:(h + 1) * dv]
        hm_ref[0, :, h * dv:(h + 1) * dv] = (_sigmoid(o_gate) * y).astype(hm_ref.dtype)

        kw = k * d_m
        c_out_ref[0, h] = g * c0 + to_col(kw) * v
        n_out_ref[0, h:h + 1, :] = g * n0 + kw
        m_new_row = jnp.where(lane == h, m, m_new_row)
    m_out_ref[0] = m_new_row


def _mlstm_decode(proj, gcol, b_ig, b_fg, mh_norm_w, c_all, n_all, m0, layer, dk, dv):
    nh = M_HEADS
    nb, width = proj.shape
    bias = jnp.pad(jnp.concatenate([b_ig, b_fg]).astype(F32), (0, LANE - 2 * nh)).reshape(1, LANE)
    kern = functools.partial(_mlstm_decode_kernel, dk=dk, dv=dv)
    c_flat = c_all.reshape(-1, nh, dk, dv)
    n_flat = n_all.reshape(-1, nh, dk)
    return pl.pallas_call(
        kern,
        grid=(nb,),
        in_specs=[pl.BlockSpec((1, 1, width), lambda b: (b, 0, 0)),
                  pl.BlockSpec((1, 1, LANE), lambda b: (b, 0, 0)),
                  pl.BlockSpec((1, LANE), lambda b: (0, 0)),
                  pl.BlockSpec((1, nh * dv), lambda b: (0, 0)),
                  pl.BlockSpec((1, nh, dk, dv), lambda b: (layer * nb + b, 0, 0, 0)),
                  pl.BlockSpec((1, nh, dk), lambda b: (layer * nb + b, 0, 0)),
                  pl.BlockSpec((1, 1, nh), lambda b: (b, 0, 0))],
        out_specs=[pl.BlockSpec((1, 1, nh * dv), lambda b: (b, 0, 0)),
                   pl.BlockSpec((1, nh, dk, dv), lambda b: (b, 0, 0, 0)),
                   pl.BlockSpec((1, nh, dk), lambda b: (b, 0, 0)),
                   pl.BlockSpec((1, 1, LANE), lambda b: (b, 0, 0))],
        out_shape=[jax.ShapeDtypeStruct((nb, 1, nh * dv), BF16),
                   jax.ShapeDtypeStruct((nb, nh, dk, dv), F32),
                   jax.ShapeDtypeStruct((nb, nh, dk), F32),
                   jax.ShapeDtypeStruct((nb, 1, LANE), F32)],
        compiler_params=_cparams(("arbitrary",)),
        name="mlstm_decode",
    )(proj.reshape(nb, 1, width), gcol.reshape(nb, 1, LANE), bias,
      mh_norm_w.reshape(1, nh * dv), c_flat, n_flat, m0.reshape(nb, 1, nh))


def _lambda_full(lq1, lk1, lq2, lk2, lam_init):
    a = jnp.exp(jnp.sum(lq1 * lk1, axis=-1, keepdims=True))
    b = jnp.exp(jnp.sum(lq2 * lk2, axis=-1, keepdims=True))
    return a - b + lam_init


def _lane_fold(x, op):
    out = x[:, :LANE]
    for i in range(1, x.shape[-1] // LANE):
        out = op(out, x[:, i * LANE:(i + 1) * LANE])
    return out


def _attn_prompt_kernel(q_ref, k_ref, v_ref, lq1_ref, lk1_ref, lq2_ref, lk2_ref, sw_ref,
                        o_ref, kb_ref, vb_ref, s_ref, p_ref, *, dh, lam_init, tq):
    seq = q_ref.shape[0]
    kb_ref[...] = k_ref[...].astype(BF16)
    vb_ref[...] = v_ref[...].astype(BF16)
    lam = _lambda_full(lq1_ref[...], lk1_ref[...], lq2_ref[...], lk2_ref[...], lam_init)
    causal = (lax.broadcasted_iota(jnp.int32, (tq, tq), 0)
              >= lax.broadcasted_iota(jnp.int32, (tq, tq), 1))
    for qi in range(seq // tq):
        kv_len = (qi + 1) * tq
        outs = []
        for c in range(2):
            q_c = q_ref[qi * tq:(qi + 1) * tq, c * dh:(c + 1) * dh]
            mx = None
            for j in range(qi + 1):
                s = _dot_nt(q_c, kb_ref[j * tq:(j + 1) * tq, c * dh:(c + 1) * dh])
                if j == qi:
                    s = jnp.where(causal, s, -jnp.inf)
                s_ref[:, j * tq:(j + 1) * tq] = s
                part = _lane_fold(s, jnp.maximum)
                mx = part if mx is None else jnp.maximum(mx, part)
            m = jnp.max(mx, axis=-1, keepdims=True)
            ls = None
            for j in range(qi + 1):
                p = jnp.exp2(s_ref[:, j * tq:(j + 1) * tq] - m)
                part = _lane_fold(p, jnp.add)
                ls = part if ls is None else ls + part
                p_ref[:, j * tq:(j + 1) * tq] = p.astype(BF16)
            l = jnp.sum(ls, axis=-1, keepdims=True)
            outs.append(_dot(p_ref[:, :kv_len], vb_ref[:kv_len, :]) / l)
        o = outs[0] - lam * outs[1]
        y = o * lax.rsqrt(jnp.mean(o * o, axis=-1, keepdims=True) + EPS) * sw_ref[...]
        o_ref[qi * tq:(qi + 1) * tq, :] = (y * (1.0 - lam_init)).astype(o_ref.dtype)


def _attn_prompt(q, k, v, lams, subln_w, batch, seq, dh, dv, lam_init):
    tq = min(ATTN_TQ, seq)
    m = batch * seq
    kern = functools.partial(_attn_prompt_kernel, dh=dh, lam_init=lam_init, tq=tq)
    lam_spec = pl.BlockSpec((1, dh), lambda b, h: (0, 0))
    return pl.pallas_call(
        kern,
        grid=(batch, A_HEADS),
        in_specs=[pl.BlockSpec((seq, 2 * dh), lambda b, h: (b, h)),
                  pl.BlockSpec((seq, 2 * dh), lambda b, h: (b, h)),
                  pl.BlockSpec((seq, dv), lambda b, h: (b, h)),
                  lam_spec, lam_spec, lam_spec, lam_spec,
                  pl.BlockSpec((1, dv), lambda b, h: (0, 0))],
        out_specs=pl.BlockSpec((seq, dv), lambda b, h: (b, h)),
        out_shape=jax.ShapeDtypeStruct((m, A_HEADS * dv), BF16),
        scratch_shapes=[pltpu.VMEM((seq, 2 * dh), BF16),
                        pltpu.VMEM((seq, dv), BF16),
                        pltpu.VMEM((tq, seq), F32),
                        pltpu.VMEM((tq, seq), BF16)],
        compiler_params=_cparams(("arbitrary", "arbitrary")),
        name="attn_prompt",
    )(q, k, v, *[l.reshape(1, dh) for l in lams], subln_w.reshape(1, dv))


def _attn_decode_kernel(pt_ref, q_ref, kn_ref, vn_ref, *rest, dh, dv, lam_init, n_steps, n_pg, page):
    kp_refs = rest[:n_pg]
    vp_refs = rest[n_pg:2 * n_pg]
    lq1_ref, lk1_ref, lq2_ref, lk2_ref, sw_ref, o_ref, acc_ref, ml_ref = rest[2 * n_pg:]
    step = pl.program_id(1)
    nh = A_HEADS
    nr = 2 * nh
    grp = 2 * LANE
    tok_g = grp // nr
    n_grp = page * nr // grp
    qs = q_ref[0] * (dh ** -0.5 * LOG2E)

    @pl.when(step == 0)
    def _():
        acc_ref[...] = jnp.zeros_like(acc_ref)
        ml_ref[0] = jnp.full(ml_ref.shape[1:], -jnp.inf, F32)
        ml_ref[1] = jnp.zeros(ml_ref.shape[1:], F32)

    def rescale(m_blk):
        m_old = ml_ref[0][:, 0:1]
        m_new = jnp.maximum(m_old, m_blk)
        return m_new, jnp.exp2(m_old - m_new)

    def commit(m_new, alpha, l_blk, pv):
        acc_ref[...] = alpha * acc_ref[...] + pv
        ml_ref[1] = jnp.broadcast_to(alpha * ml_ref[1][:, 0:1] + l_blk, ml_ref.shape[1:])
        ml_ref[0] = jnp.broadcast_to(m_new, ml_ref.shape[1:])

    shape3 = (n_pg * n_grp, nr, grp)
    r3 = lax.broadcasted_iota(jnp.int32, shape3, 1)
    c3 = lax.broadcasted_iota(jnp.int32, shape3, 2)
    valid = (c3 % nr) == (r3 % nh) * 2 + r3 // nh
    e_row = lax.broadcasted_iota(jnp.int32, (grp, LANE), 0)
    e_col = lax.broadcasted_iota(jnp.int32, (grp, LANE), 1)
    fold = ((e_row // nr == e_col // nh) & ((e_row % nr) // 2 == e_col % nh)).astype(BF16)

    qb = qs.astype(BF16)
    pieces = []
    for pg in range(n_pg):
        s_pg = _dot_nt(qb, kp_refs[pg][...].astype(BF16))
        pieces += [s_pg[:, g * grp:(g + 1) * grp] for g in range(n_grp)]
    s3 = jnp.where(valid, jnp.concatenate(pieces, axis=0).reshape(shape3), -jnp.inf)
    m_new, alpha = rescale(jnp.max(jnp.max(s3, axis=0), axis=-1, keepdims=True))
    p3 = jnp.exp2(s3 - m_new[None])
    l_blk = jnp.sum(jnp.sum(p3, axis=0), axis=-1, keepdims=True)
    p2 = _dot(p3.reshape(n_pg * n_grp * nr, grp).astype(BF16), fold).astype(BF16)
    pv = None
    for pg in range(n_pg):
        lhs = jnp.concatenate([p2[(pg * n_grp + g) * nr:(pg * n_grp + g + 1) * nr] for g in range(n_grp)], axis=1)
        part = _dot(lhs, vp_refs[pg][...].astype(BF16))
        pv = part if pv is None else pv + part
    commit(m_new, alpha, l_blk, pv)

    @pl.when(step == n_steps - 1)
    def _():
        s_new = jnp.sum(qs * kn_ref[0], axis=-1, keepdims=True)
        m_fin, a_fin = rescale(s_new)
        p_new = jnp.exp2(s_new - m_fin)
        vn = vn_ref[0]
        commit(m_fin, a_fin, p_new, p_new * jnp.concatenate([vn, vn], axis=0))
        lam = _lambda_full(lq1_ref[...], lk1_ref[...], lq2_ref[...], lk2_ref[...], lam_init)
        o_n = acc_ref[...] / ml_ref[1][:, 0:1]
        o = o_n[:nh] - lam * o_n[nh:]
        y = o * lax.rsqrt(jnp.mean(o * o, axis=-1, keepdims=True) + EPS) * sw_ref[...]
        o_ref[0] = (y * (1.0 - lam_init)).astype(o_ref.dtype)


def _attn_decode(page_ids, q, k_new, v_new, cache_k, cache_v, lams, subln_w, dh, dv, lam_init):
    nb, n_pages = page_ids.shape
    page = cache_k.shape[-4]
    nh = A_HEADS
    nr = 2 * nh
    n_pg = math.gcd(DECODE_PAGES, n_pages)
    n_steps = n_pages // n_pg
    k2 = cache_k.reshape(-1, dh)
    v2 = cache_v.reshape(-1, dv)

    def rows_ch(x):
        return x.reshape(nb, nh, 2, dh).transpose(0, 2, 1, 3).reshape(nb, nr, dh)

    kern = functools.partial(_attn_decode_kernel, dh=dh, dv=dv, lam_init=lam_init,
                             n_steps=n_steps, n_pg=n_pg, page=page)
    lam_spec = pl.BlockSpec((1, dh), lambda b, s, pt: (0, 0))
    k_specs = [pl.BlockSpec((page * nr, dh), functools.partial(lambda b, s, pt, g: (pt[b, s * n_pg + g], 0), g=g))
               for g in range(n_pg)]
    v_specs = [pl.BlockSpec((page * nh, dv), functools.partial(lambda b, s, pt, g: (pt[b, s * n_pg + g], 0), g=g))
               for g in range(n_pg)]
    grid_spec = pltpu.PrefetchScalarGridSpec(
        num_scalar_prefetch=1,
        grid=(nb, n_steps),
        in_specs=[pl.BlockSpec((1, nr, dh), lambda b, s, pt: (b, 0, 0)),
                  pl.BlockSpec((1, nr, dh), lambda b, s, pt: (b, 0, 0)),
                  pl.BlockSpec((1, nh, dv), lambda b, s, pt: (b, 0, 0)),
                  *k_specs, *v_specs,
                  lam_spec, lam_spec, lam_spec, lam_spec,
                  pl.BlockSpec((1, dv), lambda b, s, pt: (0, 0))],
        out_specs=pl.BlockSpec((1, nh, dv), lambda b, s, pt: (b, 0, 0)),
        scratch_shapes=[pltpu.VMEM((nr, dv), F32),
                        pltpu.VMEM((2, nr, LANE), F32)],
    )
    out = pl.pallas_call(
        kern,
        grid_spec=grid_spec,
        out_shape=jax.ShapeDtypeStruct((nb, nh, dv), BF16),
        compiler_params=_cparams(("arbitrary", "arbitrary")),
        name="attn_decode",
    )(page_ids, rows_ch(q), rows_ch(k_new), v_new.reshape(nb, nh, dv),
      *([k2] * n_pg), *([v2] * n_pg), *[l.reshape(1, dh) for l in lams], subln_w.reshape(1, dv))
    return out.reshape(nb, nh * dv)


def _pick(m, pref):
    return pref if m % pref == 0 else m


def _row_tiles(m, rows_per_mod):
    unit = m if rows_per_mod == 1 else rows_per_mod
    return _pick(unit, 1024), _pick(unit, 512)


def _in_proj(x, sc1, sh1, rows_per_mod, w, q_dtype, q_scale, tn, k_rows_out):
    norm1_w, wt_m, wt_a, w_g, (n_m, aq, av), q_norm_w, k_norm_w = w
    m = x.shape[0]
    tm, tr = _row_tiles(m, rows_per_mod)
    h = _norm_mod(x, norm1_w, sc1, sh1, rows_per_mod, tr)
    proj_m = _mm([(h, wt_m)], wt_row0=0, n_out=n_m, tm=tm, tn=tn, name="in_mlstm")
    gcol = _gate_proj(h, w_g, _pick(m, 1024))
    qa = _mm([(h, wt_a)], wt_row0=0, n_out=aq, epilogue="qknorm", normw=q_norm_w, post_scale=q_scale,
             out_dtype=q_dtype, tm=tm, tn=tn, name="in_aq")
    tm_k = _pick(tm, 512) if k_rows_out else tm
    ka = _mm([(h, wt_a)], wt_row0=aq, n_out=aq, epilogue="qknorm", normw=k_norm_w,
             rows_out=k_rows_out, tm=tm_k, tn=tn, name="in_ak")
    va = _mm([(h, wt_a)], wt_row0=2 * aq, n_out=av, tm=tm, tn=tn, name="in_av")
    return proj_m, gcol, qa, ka, va


def kernel(x_prompt, x_sample, cache_k, cache_v, state_C, state_n, state_m, page_table,
           c_prompt, c_sample, w_ada, b_ada, norm1_w, w_in, b_igate, b_fgate, mh_norm_w,
           q_norm_w, k_norm_w, lambda_q1, lambda_k1, lambda_q2, lambda_k2, subln_w,
           w_out, norm2_w, w_up, w_down):
    depth = w_ada.shape[0]
    batch, seq, d = x_prompt.shape
    nb = x_sample.shape[0]
    dk = state_C.shape[3]
    dv = state_C.shape[4]
    dh = cache_k.shape[-1]
    adv = cache_v.shape[-1]
    mq, mv = M_HEADS * dk, M_HEADS * dv
    aq, av = A_HEADS * 2 * dh, A_HEADS * adv
    n_pool = cache_k.shape[1]
    n_m = 2 * mq + 2 * mv
    o_attn = n_m + 2 * M_HEADS

    yp = x_prompt.reshape(batch * seq, d)
    ys = x_sample.reshape(nb, d)
    outs = [[] for _ in range(10)]
    for l in range(depth):
        lam_init = 0.8 - 0.6 * math.exp(-0.3 * l)
        lams = (lambda_q1[l], lambda_k1[l], lambda_q2[l], lambda_k2[l])

        c_all = jnp.concatenate([c_prompt, c_sample], axis=0)
        pad = (-c_all.shape[0]) % 8
        mod = _ada(jnp.pad(c_all, ((0, pad), (0, 0))), w_ada[l], b_ada[l])
        mods_p = tuple(mod[:batch, i * d:(i + 1) * d] for i in range(N_MOD))
        mods_s = tuple(mod[batch:batch + nb, i * d:(i + 1) * d] for i in range(N_MOD))

        sh1, sc1, g1, sh2, sc2, g2 = mods_s
        tr_s = _row_tiles(nb, 1)[1]
        h = _norm_mod(ys, norm1_w[l], sc1, sh1, 1, tr_s)
        w_in_t = jnp.swapaxes(w_in, 1, 2)[l]
        wt_m, proj = _cast_mm(h, w_in_t, epilogue="plain", out_dtype=F32, tn=512,
                              wt_rows=(0, o_attn), name="in_proj_m_s")
        wt_a, proj_a = _cast_mm(h, w_in_t, epilogue="plain", out_dtype=F32, tn=512,
                                wt_rows=(o_attn, 2 * aq + av), name="in_proj_a_s")
        proj_m = proj[:, :n_m]
        gcol = jnp.pad(proj[:, n_m:o_attn], ((0, 0), (0, LANE - 2 * M_HEADS)))
        qa = _qknorm_rows(proj_a[:, :aq], q_norm_w[l])
        ks = _qknorm_rows(proj_a[:, aq:2 * aq], k_norm_w[l])
        vs = proj_a[:, 2 * aq:2 * aq + av]
        hm, cs, ns, ms = _mlstm_decode(proj_m, gcol, b_igate[l], b_fgate[l], mh_norm_w[l],
                                       state_C, state_n, state_m[l], l, dk, dv)
        oa = _attn_decode(page_table + l * n_pool, qa, ks, vs, cache_k, cache_v,
                          lams, subln_w[l], dh, adv, lam_init)
        mix = jnp.concatenate([hm.reshape(nb, mv), oa], axis=1)
        w_out_b, x1 = _cast_mm(mix, w_out[l], epilogue="resid", out_dtype=F32, tn=512,
                               resid=ys, gate=g1, name="out_proj_s")
        h2 = _norm_mod(x1, norm2_w[l], sc2, sh2, 1, tr_s)
        w_up_b, u = _cast_mm(h2, w_up[l], epilogue="relu2", out_dtype=BF16, tn=512, name="ffn_up_s")
        w_down_b, ys = _cast_mm(u, w_down[l], epilogue="resid", out_dtype=F32, tn=_pick(d, 1024), tk=2048,
                                resid=x1, gate=g2, name="ffn_down_s")
        ms = ms[:, 0, :M_HEADS]

        sh1, sc1, g1, sh2, sc2, g2 = mods_p
        tm, tr = _row_tiles(batch * seq, seq)
        w_proj = (norm1_w[l], wt_m, wt_a, w_in[l][:, n_m:o_attn], (n_m, aq, av), q_norm_w[l], k_norm_w[l])
        proj_m, gcol, qa, (kp, kp_rows), vp = _in_proj(yp, sc1, sh1, seq, w_proj, BF16,
                                                             dh ** -0.5 * LOG2E, 1024, True)
        hm, cp, np_, mp = _mlstm_prompt(proj_m, gcol, b_igate[l], b_fgate[l],
                                        mh_norm_w[l], batch, seq, dk, dv)
        oa = _attn_prompt(qa, kp, vp, lams, subln_w[l], batch, seq, dh, adv, lam_init)
        x1 = _mm([(hm, w_out_b), (oa, w_out_b)], w_row0=[0, mv], epilogue="resid", resid=yp, gate=g1,
                 rows_per_gate=seq, tm=tm, tn=1024, name="out_proj")
        h2 = _norm_mod(x1, norm2_w[l], sc2, sh2, seq, tr)
        u = _mm([(h2, w_up_b)], epilogue="relu2", out_dtype=BF16, tm=tm, tn=1024, name="ffn_up")
        yp = _mm([(u, w_down_b)], epilogue="resid", resid=x1, gate=g2, rows_per_gate=seq,
                 tm=tm, tn=_pick(d, 1024), tk=4096, vmem_limit=VMEM_LIMIT_BIG, name="ffn_down")

        for lst, val in zip(outs, (kp_rows.reshape(batch, seq, A_HEADS, 2, dh),
                                   vp.reshape(batch, seq, A_HEADS, adv),
                                   cp, np_[..., 0], mp[:, 0, :M_HEADS],
                                   ks.reshape(nb, 1, A_HEADS, 2, dh), vs.reshape(nb, 1, A_HEADS, adv),
                                   cs, ns, ms)):
            lst.append(val)
    stacked = [jnp.stack(o) for o in outs]
    return (yp.reshape(batch, seq, d), ys.reshape(nb, 1, d), *stacked)
```

```python
import functools
import math

import jax
import jax.numpy as jnp
from jax import lax
from jax.experimental import pallas as pl
from jax.experimental.pallas import tpu as pltpu

F32 = jnp.float32
BF16 = jnp.bfloat16
EPS = 1e-6
LOG2E = math.log2(math.e)
LANE = 128
VMEM_LIMIT = 56 * 1024 * 1024
VMEM_LIMIT_BIG = 60 * 1024 * 1024

M_HEADS = 4
A_HEADS = 8
N_MOD = 6
M_CHUNK = 256
ATTN_TQ = 512
DECODE_PAGES = 8


def _cparams(sem, vmem_limit=VMEM_LIMIT):
    return pltpu.CompilerParams(dimension_semantics=sem, vmem_limit_bytes=vmem_limit)


def _dot(a, b):
    return jnp.dot(a, b, preferred_element_type=F32)


def _dot_nt(a, b):
    return lax.dot_general(a, b, (((1,), (1,)), ((), ())), preferred_element_type=F32)


def _dot_tn(a, b):
    return lax.dot_general(a, b, (((0,), (0,)), ((), ())), preferred_element_type=F32)


def _log_sigmoid(x):
    return jnp.minimum(x, 0.0) - jnp.log(1.0 + jnp.exp(-jnp.abs(x)))


def _sigmoid(x):
    return 1.0 / (1.0 + jnp.exp(-x))


def _ada_kernel(c_ref, w_ref, b_ref, o_ref):
    c = c_ref[...]
    a = (c * _sigmoid(c)).astype(BF16)
    o_ref[...] = _dot(a, w_ref[...].astype(BF16)) + b_ref[...]


def _ada(c, w, b, tn=512):
    m, k = c.shape
    n = w.shape[1]
    return pl.pallas_call(
        _ada_kernel,
        grid=(n // tn,),
        in_specs=[pl.BlockSpec((m, k), lambda j: (0, 0)),
                  pl.BlockSpec((k, tn), lambda j: (0, j)),
                  pl.BlockSpec((1, tn), lambda j: (0, j))],
        out_specs=pl.BlockSpec((m, tn), lambda j: (0, j)),
        out_shape=jax.ShapeDtypeStruct((m, n), F32),
        compiler_params=_cparams(("arbitrary",)),
        name="ada",
    )(c, w, b.reshape(1, n))


def _norm_mod_kernel(x_ref, w_ref, sc_ref, sh_ref, o_ref):
    x = x_ref[...]
    sc = sc_ref[...].reshape(-1, x.shape[-1])
    sh = sh_ref[...].reshape(-1, x.shape[-1])
    y = x * lax.rsqrt(jnp.mean(x * x, axis=-1, keepdims=True) + EPS) * w_ref[...]
    o_ref[...] = (y * (1.0 + sc) + sh).astype(o_ref.dtype)


def _norm_mod(x, w, sc, sh, rows_per_mod, tr):
    m, d = x.shape
    if rows_per_mod == 1:
        mod_spec = pl.BlockSpec((tr, d), lambda i: (i, 0))
    else:
        per = rows_per_mod // tr
        sc = sc.reshape(-1, 1, d)
        sh = sh.reshape(-1, 1, d)
        mod_spec = pl.BlockSpec((1, 1, d), lambda i: (i // per, 0, 0))
    return pl.pallas_call(
        _norm_mod_kernel,
        grid=(m // tr,),
        in_specs=[pl.BlockSpec((tr, d), lambda i: (i, 0)),
                  pl.BlockSpec((1, d), lambda i: (0, 0)),
                  mod_spec, mod_spec],
        out_specs=pl.BlockSpec((tr, d), lambda i: (i, 0)),
        out_shape=jax.ShapeDtypeStruct((m, d), BF16),
        compiler_params=_cparams(("arbitrary",)),
        name="norm_mod",
    )(x, w.reshape(1, d), sc, sh)


def _mm_kernel(*refs, n_pairs, epilogue, nk, post_scale, rows_out, w_nt):
    ab = refs[:2 * n_pairs]
    rest = refs[2 * n_pairs:]
    if rows_out:
        o3_ref = rest[-1]
        rest = rest[:-1]
    o_ref = rest[-1]
    extras = rest[:-1]

    def partial_product():
        dot = _dot_nt if w_nt else _dot
        r = dot(ab[0][...], ab[1][...])
        for p in range(1, n_pairs):
            r = r + dot(ab[2 * p][...], ab[2 * p + 1][...])
        return r

    def finish(acc):
        if epilogue == "plain":
            o_ref[...] = acc.astype(o_ref.dtype)
        elif epilogue == "relu2":
            r = jnp.maximum(acc, 0.0)
            o_ref[...] = (r * r).astype(o_ref.dtype)
        elif epilogue == "qknorm":
            w = extras[0][...]
            for j in range(acc.shape[-1] // LANE):
                blk = acc[:, j * LANE:(j + 1) * LANE]
                y = blk * lax.rsqrt(jnp.mean(blk * blk, axis=-1, keepdims=True) + EPS) * w
                if post_scale != 1.0:
                    y = y * post_scale
                o_ref[:, j * LANE:(j + 1) * LANE] = y.astype(o_ref.dtype)
                if rows_out:
                    o3_ref[:, j, :] = y.astype(o3_ref.dtype)
        elif epilogue == "resid":
            x = extras[0][...]
            g = extras[1][...].reshape(-1, acc.shape[-1])
            o_ref[...] = (x + g * acc).astype(o_ref.dtype)
        else:
            raise ValueError(epilogue)

    if nk == 1:
        finish(partial_product())
    else:
        k = pl.program_id(2)

        @pl.when(k == 0)
        def _():
            o_ref[...] = partial_product()

        @pl.when((k > 0) & (k < nk - 1))
        def _():
            o_ref[...] += partial_product()

        @pl.when(k == nk - 1)
        def _():
            finish(o_ref[...] + partial_product())


def _mm(pairs, *, epilogue="plain", out_dtype=F32, tm, tn, tk=None, n_out=None, w_row0=None, wt_row0=None,
        normw=None, post_scale=1.0, rows_out=False, resid=None, gate=None, rows_per_gate=1,
        vmem_limit=VMEM_LIMIT, name="mm"):
    m, kdim = pairs[0][0].shape
    w_nt = wt_row0 is not None
    n = n_out if (n_out is not None) else pairs[0][1].shape[1]
    tk = kdim if tk is None else tk
    nk = kdim // tk
    w_row0 = [0] * len(pairs) if w_row0 is None else w_row0
    assert m % tm == 0 and n % tn == 0 and kdim % tk == 0
    grid = (m // tm, n // tn, nk)
    in_specs, args = [], []
    for (a, w), r0 in zip(pairs, w_row0):
        in_specs.append(pl.BlockSpec((tm, tk), lambda i, j, k: (i, k)))
        if w_nt:
            assert len(pairs) == 1 and a.shape == (m, kdim) and w.shape[1] == kdim
            assert wt_row0 % tn == 0 and w.shape[0] >= wt_row0 + n
            in_specs.append(pl.BlockSpec((tn, tk), lambda i, j, k: (wt_row0 // tn + j, k)))
        else:
            assert a.shape == (m, kdim) and w.shape[0] >= r0 + kdim and w.shape[1] >= n and r0 % tk == 0
            in_specs.append(pl.BlockSpec((tk, tn), functools.partial(lambda i, j, k, kb: (k + kb, j), kb=r0 // tk)))
        args += [a, w]
    if epilogue == "qknorm":
        in_specs.append(pl.BlockSpec((1, LANE), lambda i, j, k: (0, 0)))
        args.append(normw.reshape(1, LANE))
    if epilogue == "resid":
        in_specs.append(pl.BlockSpec((tm, tn), lambda i, j, k: (i, j)))
        args.append(resid)
        if rows_per_gate == 1:
            in_specs.append(pl.BlockSpec((tm, tn), lambda i, j, k: (i, j)))
            args.append(gate)
        else:
            per = rows_per_gate // tm
            in_specs.append(pl.BlockSpec((1, 1, tn), lambda i, j, k: (i // per, 0, j)))
            args.append(gate.reshape(-1, 1, n))
    assert nk == 1 or out_dtype == F32
    out_specs = pl.BlockSpec((tm, tn), lambda i, j, k: (i, j))
    out_shape = jax.ShapeDtypeStruct((m, n), out_dtype)
    if rows_out:
        assert epilogue == "qknorm" and (tn // LANE) % 8 == 0
        out_specs = [out_specs, pl.BlockSpec((tm, tn // LANE, LANE), lambda i, j, k: (i, j, 0))]
        out_shape = [out_shape, jax.ShapeDtypeStruct((m, n // LANE, LANE), out_dtype)]
    return pl.pallas_call(
        functools.partial(_mm_kernel, n_pairs=len(pairs), epilogue=epilogue, nk=nk, post_scale=post_scale,
                          rows_out=rows_out, w_nt=w_nt),
        grid=grid,
        in_specs=in_specs,
        out_specs=out_specs,
        out_shape=out_shape,
        compiler_params=_cparams(("arbitrary", "arbitrary", "arbitrary"), vmem_limit),
        name=name,
    )(*args)


def _cast_mm_kernel(a_ref, w_ref, *rest, epilogue, nk, w_nt):
    if nk > 1:
        acc_ref = rest[-1]
        rest = rest[:-1]
    wb_ref, o_ref = rest[-2:]
    extras = rest[:-2]
    wb = w_ref[...].astype(BF16)
    wb_ref[...] = wb
    part = _dot_nt(a_ref[...], wb) if w_nt else _dot(a_ref[...], wb)

    def finish(acc):
        if epilogue == "plain":
            o_ref[...] = acc.astype(o_ref.dtype)
        elif epilogue == "relu2":
            r = jnp.maximum(acc, 0.0)
            o_ref[...] = (r * r).astype(o_ref.dtype)
        elif epilogue == "resid":
            o_ref[...] = (extras[0][...] + extras[1][...] * acc).astype(o_ref.dtype)
        else:
            raise ValueError(epilogue)

    if nk == 1:
        finish(part)
    else:
        k = pl.program_id(1)

        @pl.when(k == 0)
        def _():
            acc_ref[...] = part

        @pl.when(k > 0)
        def _():
            acc_ref[...] += part

        @pl.when(k == nk - 1)
        def _():
            finish(acc_ref[...])


def _cast_mm(a, w, *, epilogue, out_dtype, tn, tk=None, wt_rows=None, resid=None, gate=None, name="cast_mm"):
    m, kdim = a.shape
    w_nt = wt_rows is not None
    tk = kdim if tk is None else tk
    nk = kdim // tk
    if w_nt:
        r0, n = wt_rows
        assert w.shape[1] == kdim and r0 % 8 == 0 and r0 + pl.cdiv(n, tn) * tn <= w.shape[0]
        w_spec = pl.BlockSpec((pl.Element(tn), pl.Element(tk)),
                              lambda j, k: (pl.multiple_of(r0 + j * tn, 8), k * tk))
        wb_spec = pl.BlockSpec((tn, tk), lambda j, k: (j, k))
        wb_shape = (n, kdim)
    else:
        n = w.shape[1]
        assert w.shape[0] == kdim
        w_spec = wb_spec = pl.BlockSpec((tk, tn), lambda j, k: (k, j))
        wb_shape = (kdim, n)
    assert kdim % tk == 0 and (n % tn == 0 or epilogue == "plain")
    in_specs = [pl.BlockSpec((m, tk), lambda j, k: (0, k)), w_spec]
    args = [a, w]
    if epilogue == "resid":
        in_specs += [pl.BlockSpec((m, tn), lambda j, k: (0, j))] * 2
        args += [resid, gate]
    return pl.pallas_call(
        functools.partial(_cast_mm_kernel, epilogue=epilogue, nk=nk, w_nt=w_nt),
        grid=(pl.cdiv(n, tn), nk),
        in_specs=in_specs,
        out_specs=[wb_spec, pl.BlockSpec((m, tn), lambda j, k: (0, j))],
        out_shape=[jax.ShapeDtypeStruct(wb_shape, BF16),
                   jax.ShapeDtypeStruct((m, n), out_dtype)],
        scratch_shapes=[pltpu.VMEM((m, tn), F32)] if nk > 1 else [],
        compiler_params=_cparams(("arbitrary", "arbitrary")),
        name=name,
    )(*args)


def _qknorm_rows_kernel(x_ref, w_ref, o_ref):
    w = w_ref[...]
    for j in range(x_ref.shape[-1] // LANE):
        blk = x_ref[:, j * LANE:(j + 1) * LANE]
        o_ref[:, j * LANE:(j + 1) * LANE] = blk * lax.rsqrt(jnp.mean(blk * blk, axis=-1, keepdims=True) + EPS) * w


def _qknorm_rows(x, w):
    m, n = x.shape
    return pl.pallas_call(
        _qknorm_rows_kernel,
        grid=(1,),
        in_specs=[pl.BlockSpec((m, n), lambda i: (0, 0)),
                  pl.BlockSpec((1, LANE), lambda i: (0, 0))],
        out_specs=pl.BlockSpec((m, n), lambda i: (0, 0)),
        out_shape=jax.ShapeDtypeStruct((m, n), F32),
        compiler_params=_cparams(("arbitrary",)),
        name="qknorm_rows",
    )(x, w.reshape(1, LANE))


def _gate_kernel(h_ref, wc_ref, oc_ref):
    oc_ref[...] = _dot(h_ref[...], wc_ref[...].astype(BF16))


def _gate_proj(h, w_gate_rows, tm):
    m, k = h.shape
    wc = jnp.pad(w_gate_rows.T, ((0, 0), (0, LANE - 8)))
    return pl.pallas_call(
        _gate_kernel,
        grid=(m // tm,),
        in_specs=[pl.BlockSpec((tm, k), lambda i: (i, 0)),
                  pl.BlockSpec((k, LANE), lambda i: (0, 0))],
        out_specs=pl.BlockSpec((tm, LANE), lambda i: (i, 0)),
        out_shape=jax.ShapeDtypeStruct((m, LANE), F32),
        compiler_params=_cparams(("arbitrary",)),
        name="gate_proj",
    )(h, wc)


def _mlstm_prompt_kernel(proj_ref, gc_ref, bc_ref, nw_ref,
                         hm_ref, c_out_ref, n_out_ref, m_out_ref,
                         cext_ref, m_ref, *, dk, dv, nc):
    c_idx = pl.program_id(1)
    L = gc_ref.shape[0]
    nh = M_HEADS
    scale = dk ** -0.5

    @pl.when(c_idx == 0)
    def _():
        cext_ref[...] = jnp.zeros_like(cext_ref)
        m_ref[...] = jnp.zeros_like(m_ref)

    row = lax.broadcasted_iota(jnp.int32, (L, L), 0)
    col = lax.broadcasted_iota(jnp.int32, (L, L), 1)
    causal = row >= col
    tril = causal.astype(F32)

    gcol = gc_ref[...] + bc_ref[...]
    grow = gcol.T[:16, :]
    lf_col = _log_sigmoid(gcol)
    lf_row = _log_sigmoid(grow)
    bt_col_all = jnp.dot(tril, lf_col, preferred_element_type=F32, precision=lax.Precision.HIGHEST)
    bt_row_all = lax.dot_general(lf_row, tril, (((1,), (1,)), ((), ())),
                                 preferred_element_type=F32, precision=lax.Precision.HIGHEST)

    lane = lax.broadcasted_iota(jnp.int32, (1, LANE), 1)
    m_new_row = jnp.zeros((1, LANE), F32)
    ones_pad = jnp.ones((L, LANE), BF16)

    for h in range(nh):
        it_c = gcol[:, h:h + 1]
        bt_c = bt_col_all[:, nh + h:nh + h + 1]
        it_r = grow[h:h + 1, :]
        bt_r = bt_row_all[nh + h:nh + h + 1, :]
        m0 = m_ref[h][0:1, 0:1]

        log_d = jnp.where(causal, bt_c - bt_r + it_r, -jnp.inf)
        m_inter = bt_c + m0
        m = jnp.maximum(m_inter, jnp.max(log_d, axis=-1, keepdims=True))
        d_m = jnp.exp(log_d - m)
        g = jnp.exp(m_inter - m)

        q = proj_ref[:, h * dk:(h + 1) * dk].astype(BF16)
        k_f = proj_ref[:, nh * dk + h * dk:nh * dk + (h + 1) * dk] * scale
        v_off = 2 * nh * dk
        v = proj_ref[:, v_off + h * dv:v_off + (h + 1) * dv].astype(BF16)
        vext = jnp.concatenate([v, ones_pad], axis=-1)
        o_off = v_off + nh * dv
        o_gate = proj_ref[:, o_off + h * dv:o_off + (h + 1) * dv]

        s = _dot_nt(q, k_f.astype(BF16)) * d_m
        cext = cext_ref[h]
        numext = _dot(s.astype(BF16), vext) + _dot(q, cext.astype(BF16)) * g
        num = numext[:, :dv]
        den = numext[:, dv:dv + 1]
        den = jnp.maximum(jnp.abs(den), jnp.exp(-m))
        hh = num / den
        y = hh * lax.rsqrt(jnp.mean(hh * hh, axis=-1, keepdims=True) + EPS) * nw_ref[:, h * dv:(h + 1) * dv]
        hm_ref[:, h * dv:(h + 1) * dv] = (_sigmoid(o_gate) * y).astype(hm_ref.dtype)

        m_last = m[L - 1:L, :]
        bt_last = bt_c[L - 1:L, :]
        w_c = jnp.exp(bt_last - bt_c + it_c - m_last)
        g_last = jnp.exp(bt_last + m0 - m_last)
        kw = (k_f * w_c).astype(BF16)
        cext_ref[h] = g_last * cext + _dot_tn(kw, vext)
        m_ref[h] = jnp.broadcast_to(m_last, m_ref.shape[1:])
        m_new_row = jnp.where(lane == h, m_last, m_new_row)

    @pl.when(c_idx == nc - 1)
    def _():
        for h in range(nh):
            c_out_ref[0, h] = cext_ref[h][:, :dv]
            n_out_ref[0, h] = cext_ref[h][:, dv:]
        m_out_ref[0] = m_new_row


def _mlstm_prompt(proj, gcol, b_ig, b_fg, mh_norm_w, batch, seq, dk, dv):
    nh = M_HEADS
    L = min(M_CHUNK, seq)
    nc = seq // L
    m = batch * seq
    width = proj.shape[1]
    bias = jnp.concatenate([b_ig, b_fg]).astype(F32)
    bias_cols = jnp.pad(bias, (0, LANE - 2 * nh)).reshape(1, LANE)
    kern = functools.partial(_mlstm_prompt_kernel, dk=dk, dv=dv, nc=nc)
    return pl.pallas_call(
        kern,
        grid=(batch, nc),
        in_specs=[pl.BlockSpec((L, width), lambda b, c: (b * nc + c, 0)),
                  pl.BlockSpec((L, LANE), lambda b, c: (b * nc + c, 0)),
                  pl.BlockSpec((1, LANE), lambda b, c: (0, 0)),
                  pl.BlockSpec((1, nh * dv), lambda b, c: (0, 0))],
        out_specs=[pl.BlockSpec((L, nh * dv), lambda b, c: (b * nc + c, 0)),
                   pl.BlockSpec((1, nh, dk, dv), lambda b, c: (b, 0, 0, 0)),
                   pl.BlockSpec((1, nh, dk, LANE), lambda b, c: (b, 0, 0, 0)),
                   pl.BlockSpec((1, 1, LANE), lambda b, c: (b, 0, 0))],
        out_shape=[jax.ShapeDtypeStruct((m, nh * dv), BF16),
                   jax.ShapeDtypeStruct((batch, nh, dk, dv), F32),
                   jax.ShapeDtypeStruct((batch, nh, dk, LANE), F32),
                   jax.ShapeDtypeStruct((batch, 1, LANE), F32)],
        scratch_shapes=[pltpu.VMEM((nh, dk, dv + LANE), F32),
                        pltpu.VMEM((nh, 8, LANE), F32)],
        compiler_params=_cparams(("arbitrary", "arbitrary")),
        name="mlstm_prompt",
    )(proj, gcol, bias_cols, mh_norm_w.reshape(1, nh * dv))


def _mlstm_decode_kernel(proj_ref, g_ref, bias_ref, nw_ref, c0_ref, n0_ref, m0_ref,
                         hm_ref, c_out_ref, n_out_ref, m_out_ref, *, dk, dv):
    nh = M_HEADS
    scale = dk ** -0.5
    gates = g_ref[0] + bias_ref[...]
    lf_all = _log_sigmoid(gates)
    m0_all = m0_ref[0]
    lane = lax.broadcasted_iota(jnp.int32, (1, LANE), 1)
    eye = (lax.broadcasted_iota(jnp.int32, (dk, dk), 0)
           == lax.broadcasted_iota(jnp.int32, (dk, dk), 1))
    m_new_row = jnp.zeros((1, LANE), F32)

    def to_col(r):
        return jnp.sum(jnp.where(eye, jnp.broadcast_to(r, (dk, dk)), 0.0), axis=1, keepdims=True)

    v_off = 2 * nh * dk
    o_off = v_off + nh * dv
    for h in range(nh):
        it = gates[:, h:h + 1]
        lf = lf_all[:, nh + h:nh + h + 1]
        m0 = m0_all[:, h:h + 1]
        q = proj_ref[0, :, h * dk:(h + 1) * dk]
        k = proj_ref[0, :, nh * dk + h * dk:nh * dk + (h + 1) * dk] * scale
        v = proj_ref[0, :, v_off + h * dv:v_off + (h + 1) * dv]
        o_gate = proj_ref[0, :, o_off + h * dv:o_off + (h + 1) * dv]
        c0 = c0_ref[0, h]
        n0 = n0_ref[0, h:h + 1, :]

        m_inter = lf + m0
        m = jnp.maximum(m_inter, it)
        d_m = jnp.exp(it - m)
        g = jnp.exp(m_inter - m)
        s = jnp.sum(q * k, axis=-1, keepdims=True) * d_m
        q_col = to_col(q)
        qc = jnp.sum(q_col * c0, axis=0, keepdims=True)
        num = s * v + qc * g
        den = s + jnp.sum(q * n0, axis=-1, keepdims=True) * g
        den = jnp.maximum(jnp.abs(den), jnp.exp(-m))
        hh = num / den
        y = hh * lax.rsqrt(jnp.mean(hh * hh, axis=-1, keepdims=True) + EPS) * nw_ref[:, h * dv:(h + 1) * dv]
        hm_ref[0, :, h * dv:(h + 1) * dv] = (_sigmoid(o_gate) * y).astype(hm_ref.dtype)

        kw = k * d_m
        c_out_ref[0, h] = g * c0 + to_col(kw) * v
        n_out_ref[0, h:h + 1, :] = g * n0 + kw
        m_new_row = jnp.where(lane == h, m, m_new_row)
    m_out_ref[0] = m_new_row


def _mlstm_decode(proj, gcol, b_ig, b_fg, mh_norm_w, c_all, n_all, m0, layer, dk, dv):
    nh = M_HEADS
    nb, width = proj.shape
    bias = jnp.pad(jnp.concatenate([b_ig, b_fg]).astype(F32), (0, LANE - 2 * nh)).reshape(1, LANE)
    kern = functools.partial(_mlstm_decode_kernel, dk=dk, dv=dv)
    c_flat = c_all.reshape(-1, nh, dk, dv)
    n_flat = n_all.reshape(-1, nh, dk)
    return pl.pallas_call(
        kern,
        grid=(nb,),
        in_specs=[pl.BlockSpec((1, 1, width), lambda b: (b, 0, 0)),
                  pl.BlockSpec((1, 1, LANE), lambda b: (b, 0, 0)),
                  pl.BlockSpec((1, LANE), lambda b: (0, 0)),
                  pl.BlockSpec((1, nh * dv), lambda b: (0, 0)),
                  pl.BlockSpec((1, nh, dk, dv), lambda b: (layer * nb + b, 0, 0, 0)),
                  pl.BlockSpec((1, nh, dk), lambda b: (layer * nb + b, 0, 0)),
                  pl.BlockSpec((1, 1, nh), lambda b: (b, 0, 0))],
        out_specs=[pl.BlockSpec((1, 1, nh * dv), lambda b: (b, 0, 0)),
                   pl.BlockSpec((1, nh, dk, dv), lambda b: (b, 0, 0, 0)),
                   pl.BlockSpec((1, nh, dk), lambda b: (b, 0, 0)),
                   pl.BlockSpec((1, 1, LANE), lambda b: (b, 0, 0))],
        out_shape=[jax.ShapeDtypeStruct((nb, 1, nh * dv), BF16),
                   jax.ShapeDtypeStruct((nb, nh, dk, dv), F32),
                   jax.ShapeDtypeStruct((nb, nh, dk), F32),
                   jax.ShapeDtypeStruct((nb, 1, LANE), F32)],
        compiler_params=_cparams(("arbitrary",)),
        name="mlstm_decode",
    )(proj.reshape(nb, 1, width), gcol.reshape(nb, 1, LANE), bias,
      mh_norm_w.reshape(1, nh * dv), c_flat, n_flat, m0.reshape(nb, 1, nh))


def _lambda_full(lq1, lk1, lq2, lk2, lam_init):
    a = jnp.exp(jnp.sum(lq1 * lk1, axis=-1, keepdims=True))
    b = jnp.exp(jnp.sum(lq2 * lk2, axis=-1, keepdims=True))
    return a - b + lam_init


def _lane_fold(x, op):
    out = x[:, :LANE]
    for i in range(1, x.shape[-1] // LANE):
        out = op(out, x[:, i * LANE:(i + 1) * LANE])
    return out


def _attn_prompt_kernel(q_ref, k_ref, v_ref, lq1_ref, lk1_ref, lq2_ref, lk2_ref, sw_ref,
                        o_ref, kb_ref, vb_ref, s_ref, p_ref, *, dh, lam_init, tq):
    seq = q_ref.shape[0]
    kb_ref[...] = k_ref[...].astype(BF16)
    vb_ref[...] = v_ref[...].astype(BF16)
    lam = _lambda_full(lq1_ref[...], lk1_ref[...], lq2_ref[...], lk2_ref[...], lam_init)
    causal = (lax.broadcasted_iota(jnp.int32, (tq, tq), 0)
              >= lax.broadcasted_iota(jnp.int32, (tq, tq), 1))
    for qi in range(seq // tq):
        kv_len = (qi + 1) * tq
        outs = []
        for c in range(2):
            q_c = q_ref[qi * tq:(qi + 1) * tq, c * dh:(c + 1) * dh]
            mx = None
            for j in range(qi + 1):
                s = _dot_nt(q_c, kb_ref[j * tq:(j + 1) * tq, c * dh:(c + 1) * dh])
                if j == qi:
                    s = jnp.where(causal, s, -jnp.inf)
                s_ref[:, j * tq:(j + 1) * tq] = s
                part = _lane_fold(s, jnp.maximum)
                mx = part if mx is None else jnp.maximum(mx, part)
            m = jnp.max(mx, axis=-1, keepdims=True)
            ls = None
            for j in range(qi + 1):
                p = jnp.exp2(s_ref[:, j * tq:(j + 1) * tq] - m)
                part = _lane_fold(p, jnp.add)
                ls = part if ls is None else ls + part
                p_ref[:, j * tq:(j + 1) * tq] = p.astype(BF16)
            l = jnp.sum(ls, axis=-1, keepdims=True)
            outs.append(_dot(p_ref[:, :kv_len], vb_ref[:kv_len, :]) / l)
        o = outs[0] - lam * outs[1]
        y = o * lax.rsqrt(jnp.mean(o * o, axis=-1, keepdims=True) + EPS) * sw_ref[...]
        o_ref[qi * tq:(qi + 1) * tq, :] = (y * (1.0 - lam_init)).astype(o_ref.dtype)


def _attn_prompt(q, k, v, lams, subln_w, batch, seq, dh, dv, lam_init):
    tq = min(ATTN_TQ, seq)
    m = batch * seq
    kern = functools.partial(_attn_prompt_kernel, dh=dh, lam_init=lam_init, tq=tq)
    lam_spec = pl.BlockSpec((1, dh), lambda b, h: (0, 0))
    return pl.pallas_call(
        kern,
        grid=(batch, A_HEADS),
        in_specs=[pl.BlockSpec((seq, 2 * dh), lambda b, h: (b, h)),
                  pl.BlockSpec((seq, 2 * dh), lambda b, h: (b, h)),
                  pl.BlockSpec((seq, dv), lambda b, h: (b, h)),
                  lam_spec, lam_spec, lam_spec, lam_spec,
                  pl.BlockSpec((1, dv), lambda b, h: (0, 0))],
        out_specs=pl.BlockSpec((seq, dv), lambda b, h: (b, h)),
        out_shape=jax.ShapeDtypeStruct((m, A_HEADS * dv), BF16),
        scratch_shapes=[pltpu.VMEM((seq, 2 * dh), BF16),
                        pltpu.VMEM((seq, dv), BF16),
                        pltpu.VMEM((tq, seq), F32),
                        pltpu.VMEM((tq, seq), BF16)],
        compiler_params=_cparams(("arbitrary", "arbitrary")),
        name="attn_prompt",
    )(q, k, v, *[l.reshape(1, dh) for l in lams], subln_w.reshape(1, dv))


def _attn_decode_kernel(pt_ref, q_ref, kn_ref, vn_ref, *rest, dh, dv, lam_init, n_steps, n_pg, page):
    kp_refs = rest[:n_pg]
    vp_refs = rest[n_pg:2 * n_pg]
    lq1_ref, lk1_ref, lq2_ref, lk2_ref, sw_ref, o_ref, acc_ref, ml_ref = rest[2 * n_pg:]
    step = pl.program_id(1)
    nh = A_HEADS
    nr = 2 * nh
    grp = 2 * LANE
    tok_g = grp // nr
    n_grp = page * nr // grp
    qs = q_ref[0] * (dh ** -0.5 * LOG2E)

    @pl.when(step == 0)
    def _():
        acc_ref[...] = jnp.zeros_like(acc_ref)
        ml_ref[0] = jnp.full(ml_ref.shape[1:], -jnp.inf, F32)
        ml_ref[1] = jnp.zeros(ml_ref.shape[1:], F32)

    def rescale(m_blk):
        m_old = ml_ref[0][:, 0:1]
        m_new = jnp.maximum(m_old, m_blk)
        return m_new, jnp.exp2(m_old - m_new)

    def commit(m_new, alpha, l_blk, pv):
        acc_ref[...] = alpha * acc_ref[...] + pv
        ml_ref[1] = jnp.broadcast_to(alpha * ml_ref[1][:, 0:1] + l_blk, ml_ref.shape[1:])
        ml_ref[0] = jnp.broadcast_to(m_new, ml_ref.shape[1:])

    shape3 = (n_pg * n_grp, nr, grp)
    r3 = lax.broadcasted_iota(jnp.int32, shape3, 1)
    c3 = lax.broadcasted_iota(jnp.int32, shape3, 2)
    valid = (c3 % nr) == (r3 % nh) * 2 + r3 // nh
    e_row = lax.broadcasted_iota(jnp.int32, (grp, LANE), 0)
    e_col = lax.broadcasted_iota(jnp.int32, (grp, LANE), 1)
    fold = ((e_row // nr == e_col // nh) & ((e_row % nr) // 2 == e_col % nh)).astype(BF16)

    qb = qs.astype(BF16)
    pieces = []
    for pg in range(n_pg):
        s_pg = _dot_nt(qb, kp_refs[pg][...].astype(BF16))
        pieces += [s_pg[:, g * grp:(g + 1) * grp] for g in range(n_grp)]
    s3 = jnp.where(valid, jnp.concatenate(pieces, axis=0).reshape(shape3), -jnp.inf)
    m_new, alpha = rescale(jnp.max(jnp.max(s3, axis=0), axis=-1, keepdims=True))
    p3 = jnp.exp2(s3 - m_new[None])
    l_blk = jnp.sum(jnp.sum(p3, axis=0), axis=-1, keepdims=True)
    p2 = _dot(p3.reshape(n_pg * n_grp * nr, grp).astype(BF16), fold).astype(BF16)
    pv = None
    for pg in range(n_pg):
        lhs = jnp.concatenate([p2[(pg * n_grp + g) * nr:(pg * n_grp + g + 1) * nr] for g in range(n_grp)], axis=1)
        part = _dot(lhs, vp_refs[pg][...].astype(BF16))
        pv = part if pv is None else pv + part
    commit(m_new, alpha, l_blk, pv)

    @pl.when(step == n_steps - 1)
    def _():
        s_new = jnp.sum(qs * kn_ref[0], axis=-1, keepdims=True)
        m_fin, a_fin = rescale(s_new)
        p_new = jnp.exp2(s_new - m_fin)
        vn = vn_ref[0]
        commit(m_fin, a_fin, p_new, p_new * jnp.concatenate([vn, vn], axis=0))
        lam = _lambda_full(lq1_ref[...], lk1_ref[...], lq2_ref[...], lk2_ref[...], lam_init)
        o_n = acc_ref[...] / ml_ref[1][:, 0:1]
        o = o_n[:nh] - lam * o_n[nh:]
        y = o * lax.rsqrt(jnp.mean(o * o, axis=-1, keepdims=True) + EPS) * sw_ref[...]
        o_ref[0] = (y * (1.0 - lam_init)).astype(o_ref.dtype)


def _attn_decode(page_ids, q, k_new, v_new, cache_k, cache_v, lams, subln_w, dh, dv, lam_init):
    nb, n_pages = page_ids.shape
    page = cache_k.shape[-4]
    nh = A_HEADS
    nr = 2 * nh
    n_pg = math.gcd(DECODE_PAGES, n_pages)
    n_steps = n_pages // n_pg
    k2 = cache_k.reshape(-1, dh)
    v2 = cache_v.reshape(-1, dv)

    def rows_ch(x):
        return x.reshape(nb, nh, 2, dh).transpose(0, 2, 1, 3).reshape(nb, nr, dh)

    kern = functools.partial(_attn_decode_kernel, dh=dh, dv=dv, lam_init=lam_init,
                             n_steps=n_steps, n_pg=n_pg, page=page)
    lam_spec = pl.BlockSpec((1, dh), lambda b, s, pt: (0, 0))
    k_specs = [pl.BlockSpec((page * nr, dh), functools.partial(lambda b, s, pt, g: (pt[b, s * n_pg + g], 0), g=g))
               for g in range(n_pg)]
    v_specs = [pl.BlockSpec((page * nh, dv), functools.partial(lambda b, s, pt, g: (pt[b, s * n_pg + g], 0), g=g))
               for g in range(n_pg)]
    grid_spec = pltpu.PrefetchScalarGridSpec(
        num_scalar_prefetch=1,
        grid=(nb, n_steps),
        in_specs=[pl.BlockSpec((1, nr, dh), lambda b, s, pt: (b, 0, 0)),
                  pl.BlockSpec((1, nr, dh), lambda b, s, pt: (b, 0, 0)),
                  pl.BlockSpec((1, nh, dv), lambda b, s, pt: (b, 0, 0)),
                  *k_specs, *v_specs,
                  lam_spec, lam_spec, lam_spec, lam_spec,
                  pl.BlockSpec((1, dv), lambda b, s, pt: (0, 0))],
        out_specs=pl.BlockSpec((1, nh, dv), lambda b, s, pt: (b, 0, 0)),
        scratch_shapes=[pltpu.VMEM((nr, dv), F32),
                        pltpu.VMEM((2, nr, LANE), F32)],
    )
    out = pl.pallas_call(
        kern,
        grid_spec=grid_spec,
        out_shape=jax.ShapeDtypeStruct((nb, nh, dv), BF16),
        compiler_params=_cparams(("arbitrary", "arbitrary")),
        name="attn_decode",
    )(page_ids, rows_ch(q), rows_ch(k_new), v_new.reshape(nb, nh, dv),
      *([k2] * n_pg), *([v2] * n_pg), *[l.reshape(1, dh) for l in lams], subln_w.reshape(1, dv))
    return out.reshape(nb, nh * dv)


def _pick(m, pref):
    return pref if m % pref == 0 else m


def _row_tiles(m, rows_per_mod):
    unit = m if rows_per_mod == 1 else rows_per_mod
    return _pick(unit, 1024), _pick(unit, 512)


def _in_proj(x, sc1, sh1, rows_per_mod, w, q_dtype, q_scale, tn, k_rows_out):
    norm1_w, wt_m, wt_a, w_g, (n_m, aq, av), q_norm_w, k_norm_w = w
    m = x.shape[0]
    tm, tr = _row_tiles(m, rows_per_mod)
    h = _norm_mod(x, norm1_w, sc1, sh1, rows_per_mod, tr)
    proj_m = _mm([(h, wt_m)], wt_row0=0, n_out=n_m, tm=tm, tn=tn, name="in_mlstm")
    gcol = _gate_proj(h, w_g, _pick(m, 1024))
    qa = _mm([(h, wt_a)], wt_row0=0, n_out=aq, epilogue="qknorm", normw=q_norm_w, post_scale=q_scale,
             out_dtype=q_dtype, tm=tm, tn=tn, name="in_aq")
    tm_k = _pick(tm, 512) if k_rows_out else tm
    ka = _mm([(h, wt_a)], wt_row0=aq, n_out=aq, epilogue="qknorm", normw=k_norm_w,
             rows_out=k_rows_out, tm=tm_k, tn=tn, name="in_ak")
    va = _mm([(h, wt_a)], wt_row0=2 * aq, n_out=av, tm=tm, tn=tn, name="in_av")
    return proj_m, gcol, qa, ka, va


def kernel(x_prompt, x_sample, cache_k, cache_v, state_C, state_n, state_m, page_table,
           c_prompt, c_sample, w_ada, b_ada, norm1_w, w_in, b_igate, b_fgate, mh_norm_w,
           q_norm_w, k_norm_w, lambda_q1, lambda_k1, lambda_q2, lambda_k2, subln_w,
           w_out, norm2_w, w_up, w_down):
    depth = w_ada.shape[0]
    batch, seq, d = x_prompt.shape
    nb = x_sample.shape[0]
    dk = state_C.shape[3]
    dv = state_C.shape[4]
    dh = cache_k.shape[-1]
    adv = cache_v.shape[-1]
    mq, mv = M_HEADS * dk, M_HEADS * dv
    aq, av = A_HEADS * 2 * dh, A_HEADS * adv
    n_pool = cache_k.shape[1]
    n_m = 2 * mq + 2 * mv
    o_attn = n_m + 2 * M_HEADS

    yp = x_prompt.reshape(batch * seq, d)
    ys = x_sample.reshape(nb, d)
    outs = [[] for _ in range(10)]
    for l in range(depth):
        lam_init = 0.8 - 0.6 * math.exp(-0.3 * l)
        lams = (lambda_q1[l], lambda_k1[l], lambda_q2[l], lambda_k2[l])

        c_all = jnp.concatenate([c_prompt, c_sample], axis=0)
        pad = (-c_all.shape[0]) % 8
        mod = _ada(jnp.pad(c_all, ((0, pad), (0, 0))), w_ada[l], b_ada[l])
        mods_p = tuple(mod[:batch, i * d:(i + 1) * d] for i in range(N_MOD))
        mods_s = tuple(mod[batch:batch + nb, i * d:(i + 1) * d] for i in range(N_MOD))

        sh1, sc1, g1, sh2, sc2, g2 = mods_s
        tr_s = _row_tiles(nb, 1)[1]
        h = _norm_mod(ys, norm1_w[l], sc1, sh1, 1, tr_s)
        w_in_t = jnp.swapaxes(w_in, 1, 2)[l]
        wt_m, proj = _cast_mm(h, w_in_t, epilogue="plain", out_dtype=F32, tn=512,
                              wt_rows=(0, o_attn), name="in_proj_m_s")
        wt_a, proj_a = _cast_mm(h, w_in_t, epilogue="plain", out_dtype=F32, tn=512,
                                wt_rows=(o_attn, 2 * aq + av), name="in_proj_a_s")
        proj_m = proj[:, :n_m]
        gcol = jnp.pad(proj[:, n_m:o_attn], ((0, 0), (0, LANE - 2 * M_HEADS)))
        qa = _qknorm_rows(proj_a[:, :aq], q_norm_w[l])
        ks = _qknorm_rows(proj_a[:, aq:2 * aq], k_norm_w[l])
        vs = proj_a[:, 2 * aq:2 * aq + av]
        hm, cs, ns, ms = _mlstm_decode(proj_m, gcol, b_igate[l], b_fgate[l], mh_norm_w[l],
                                       state_C, state_n, state_m[l], l, dk, dv)
        oa = _attn_decode(page_table + l * n_pool, qa, ks, vs, cache_k, cache_v,
                          lams, subln_w[l], dh, adv, lam_init)
        mix = jnp.concatenate([hm.reshape(nb, mv), oa], axis=1)
        w_out_b, x1 = _cast_mm(mix, w_out[l], epilogue="resid", out_dtype=F32, tn=512,
                               resid=ys, gate=g1, name="out_proj_s")
        h2 = _norm_mod(x1, norm2_w[l], sc2, sh2, 1, tr_s)
        w_up_b, u = _cast_mm(h2, w_up[l], epilogue="relu2", out_dtype=BF16, tn=512, name="ffn_up_s")
        w_down_b, ys = _cast_mm(u, w_down[l], epilogue="resid", out_dtype=F32, tn=_pick(d, 1024), tk=2048,
                                resid=x1, gate=g2, name="ffn_down_s")
        ms = ms[:, 0, :M_HEADS]

        sh1, sc1, g1, sh2, sc2, g2 = mods_p
        tm, tr = _row_tiles(batch * seq, seq)
        w_proj = (norm1_w[l], wt_m, wt_a, w_in_t[n_m:o_attn], (n_m, aq, av), q_norm_w[l], k_norm_w[l])
        proj_m, gcol, qa, (kp, kp_rows), vp = _in_proj(yp, sc1, sh1, seq, w_proj, BF16,
                                                             dh ** -0.5 * LOG2E, 1024, True)
        hm, cp, np_, mp = _mlstm_prompt(proj_m, gcol, b_igate[l], b_fgate[l],
                                        mh_norm_w[l], batch, seq, dk, dv)
        oa = _attn_prompt(qa, kp, vp, lams, subln_w[l], batch, seq, dh, adv, lam_init)
        x1 = _mm([(hm, w_out_b), (oa, w_out_b)], w_row0=[0, mv], epilogue="resid", resid=yp, gate=g1,
                 rows_per_gate=seq, tm=tm, tn=1024, name="out_proj")
        h2 = _norm_mod(x1, norm2_w[l], sc2, sh2, seq, tr)
        u = _mm([(h2, w_up_b)], epilogue="relu2", out_dtype=BF16, tm=tm, tn=1024, name="ffn_up")
        yp = _mm([(u, w_down_b)], epilogue="resid", resid=x1, gate=g2, rows_per_gate=seq,
                 tm=tm, tn=_pick(d, 1024), tk=4096, vmem_limit=VMEM_LIMIT_BIG, name="ffn_down")

        for lst, val in zip(outs, (kp_rows.reshape(batch, seq, A_HEADS, 2, dh),
                                   vp.reshape(batch, seq, A_HEADS, adv),
                                   cp, np_[..., 0], mp[:, 0, :M_HEADS],
                                   ks.reshape(nb, 1, A_HEADS, 2, dh), vs.reshape(nb, 1, A_HEADS, adv),
                                   cs, ns, ms)):
            lst.append(val)
    stacked = [jnp.stack(o) for o in outs]
    return (yp.reshape(batch, seq, d), ys.reshape(nb, 1, d), *stacked)
```
